```python
import math
import jax
import jax.numpy as jnp
from jax import lax
import numpy as np


D_MODEL = 4096
BATCH = 4
SEQ = 4096
DEPTH = 2

GRID_W = 64
CTX_LEN = 256
HEAD_DIM = 128
N_MIX_HEADS = D_MODEL // HEAD_DIM
HY_GROUPS = N_MIX_HEADS // 4
RET_HEADS = (N_MIX_HEADS - HY_GROUPS) // 2
NA_HEADS = N_MIX_HEADS - HY_GROUPS - RET_HEADS
HY_W = HY_GROUPS * HEAD_DIM
RET_W = RET_HEADS * HEAD_DIM
NA_W = NA_HEADS * HEAD_DIM
K_RK = 0
K_RV = K_RK + RET_W
K_NK = K_RV + RET_W
K_NV = K_NK + NA_W
KV_COLS = K_NV + NA_W
R_HY = 0
R_RQ = R_HY + 3 * HY_W
R_RG = R_RQ + RET_W
R_NQ = R_RG + RET_W
REST_COLS = R_NQ + NA_W
IN_COLS = KV_COLS + REST_COLS
N_MOD = 6
HY_ORDER = 2
HY_SHORT = 3
HY_EMB = 33
HY_BANDS = (HY_EMB - 1) // 2
HY_FILTER_HIDDEN = 64
HY_DECAY_TARGET = 1e-2
HY_SHORT_DECAY_PCT = 0.3
HY_LONG_DECAY_PCT = 1.5
HY_MAX_DECAY = math.log(HY_DECAY_TARGET) / HY_SHORT_DECAY_PCT
HY_MIN_DECAY = math.log(HY_DECAY_TARGET) / HY_LONG_DECAY_PCT
HY_FILTER_SCALE = 0.05
RET_CHUNK = 128
RET_SCALE = HEAD_DIM ** -0.5
ROPE_BASE = 10000.0
NA_KH = 8
NA_KW = 16
PEER_HEADS = 8
PEER_NKEYS = 128
PEER_N = PEER_NKEYS * PEER_NKEYS
PEER_TOPK = 16
PEER_DK = 256
PEER_BLOCK = 64
EPS = 1e-6
NEG_INF = -1e30

kernel_name = 'hybrid_hyena_retnet_natten_peer_dit'


def rmsnorm(x, g):
    xf = x.astype(jnp.float32)
    y = xf * lax.rsqrt(jnp.mean(xf * xf, axis=-1, keepdims=True) + EPS)
    return (y * g.astype(jnp.float32)).astype(x.dtype)


def head_rms(x):
    xf = x.astype(jnp.float32)
    return xf * lax.rsqrt(jnp.mean(xf * xf, axis=-1, keepdims=True) + EPS)


def modulate(h, shift, scale):
    return h * (1.0 + scale) + shift


def split_heads(x):
    return x.reshape(x.shape[:-1] + (x.shape[-1] // HEAD_DIM, HEAD_DIM))


def to_bhld(x):
    return jnp.swapaxes(x, 1, 2).astype(jnp.float32)


def axial_rope(x):
    L = x.shape[1]
    t = jnp.arange(L)
    row = (t // GRID_W).astype(jnp.float32)
    col = (t % GRID_W).astype(jnp.float32)
    quarter = x.shape[-1] // 4
    inv = ROPE_BASE ** (-jnp.arange(quarter, dtype=jnp.float32) / quarter)

    def rot(xp, pos):
        ang = pos[:, None] * inv[None, :]
        cos = jnp.cos(ang)[None, :, None, :]
        sin = jnp.sin(ang)[None, :, None, :]
        a, b = xp[..., :quarter], xp[..., quarter:]
        return jnp.concatenate([a * cos - b * sin, a * sin + b * cos], axis=-1)

    half = 2 * quarter
    return jnp.concatenate([rot(x[..., :half], row), rot(x[..., half:], col)], axis=-1).astype(x.dtype)


def short_conv(u, w, b):
    C = u.shape[-1]
    y = lax.conv_general_dilated(u, w[:, None, :].astype(u.dtype), window_strides=(1,),
                                 padding=[(HY_SHORT // 2, HY_SHORT // 2)],
                                 dimension_numbers=('NWC', 'WIO', 'NWC'),
                                 feature_group_count=C)
    return y + b.astype(u.dtype)


def hyena_filters(L, w1, b1, freq, w2, b2, w3, b3, w4):
    f32 = lambda a: a.astype(jnp.float32)
    t = jnp.linspace(0.0, 1.0, L, dtype=jnp.float32)[:, None]
    w = 2.0 * math.pi * jnp.arange(L, dtype=jnp.float32)[:, None] / L
    f = jnp.linspace(1e-4, HY_BANDS - 1, HY_BANDS, dtype=jnp.float32)[None, :]
    z = jnp.concatenate([t, jnp.cos(f * w), -jnp.sin(f * w)], axis=-1)
    h = jnp.sin(f32(freq) * (z @ f32(w1) + f32(b1)))
    h = jnp.sin(f32(freq) * (h @ f32(w2) + f32(b2)))
    h = jnp.sin(f32(freq) * (h @ f32(w3) + f32(b3)))
    k = (h @ f32(w4)).reshape(L, HY_ORDER, 2, HY_W)
    deltas = jnp.abs(jnp.linspace(HY_MIN_DECAY, HY_MAX_DECAY, HY_W, dtype=jnp.float32))
    return k * jnp.exp(-t[:, :, None, None] * deltas)


def long_conv_bidir(u, kf, kb, skip):
    L, C = u.shape[1], u.shape[2]
    filt = jnp.concatenate([kf, jnp.zeros((1, C), jnp.float32), kb[:0:-1]], axis=0)
    uf = u.astype(jnp.float32)
    U = jnp.fft.rfft(uf, n=2 * L, axis=1)
    K = jnp.fft.rfft(filt, n=2 * L, axis=0)
    y = jnp.fft.irfft(U * K[None], n=2 * L, axis=1)[:, :L]
    return (y + uf * skip.astype(jnp.float32)).astype(u.dtype)


def hyena_mixer(u, conv_w, conv_b, filt, skip):
    L = u.shape[1]
    u = short_conv(u, conv_w, conv_b)
    v, x1, x2 = jnp.split(u, 3, axis=-1)
    k = hyena_filters(L, *filt)
    z = x1 * long_conv_bidir(v, k[:, 0, 0], k[:, 0, 1], skip[0])
    z = x2 * long_conv_bidir(z, k[:, 1, 0], k[:, 1, 1], skip[1])
    return z


def ret_chunkwise(q, k, v, log_g, s0):
    B, H, L, d = q.shape
    C = RET_CHUNK
    N = L // C
    pos = jnp.arange(C, dtype=jnp.float32)
    diff = pos[:, None] - pos[None, :]
    inner_decay = jnp.where(diff >= 0, jnp.exp(log_g[:, None, None] * jnp.maximum(diff, 0.0)), 0.0)
    q_decay = jnp.exp(log_g[:, None] * (pos + 1.0))[:, :, None]
    k_decay = jnp.exp(log_g[:, None] * (C - 1.0 - pos))[:, :, None]
    chunk_decay = jnp.exp(log_g * C)[:, None, None]

    def chunks(a):
        return jnp.moveaxis(a.reshape(B, H, N, C, d), 2, 0)

    def step(S, qkv):
        qn, kn, vn = qkv
        a = jnp.einsum('bhid,bhjd->bhij', qn, kn) * inner_decay
        o = jnp.einsum('bhij,bhje->bhie', a, vn) + jnp.einsum('bhid,bhde->bhie', qn * q_decay, S)
        S = S * chunk_decay + jnp.einsum('bhjd,bhje->bhde', kn * k_decay, vn)
        return S, o

    _, o = lax.scan(step, s0, (chunks(q), chunks(k), chunks(v)))
    return jnp.moveaxis(o, 0, 2).reshape(B, H, L, d)


def ret_final_state(k, v, log_g, reverse):
    L = k.shape[2]
    pos = jnp.arange(L, dtype=jnp.float32)
    expo = pos if reverse else (L - 1.0 - pos)
    w = jnp.exp(log_g[:, None] * expo)
    return jnp.einsum('bhld,bhle->bhde', k * w[None, :, :, None], v)


def retention_bidir(q, k, v, log_f, log_b, s_f, s_b):
    fwd = ret_chunkwise(q, k, v, log_f, s_f)
    bwd = jnp.flip(ret_chunkwise(jnp.flip(q, 2), jnp.flip(k, 2), jnp.flip(v, 2), log_b, s_b), 2)
    return fwd + bwd


def na_latent(q, k, v, kc, vc, rpb):
    B, L, H, d = q.shape
    rows = L // GRID_W
    kh = min(NA_KH, rows)
    kw = NA_KW
    scale = d ** -0.5
    col = jnp.arange(GRID_W)
    cstart = jnp.clip(col - kw // 2, 0, GRID_W - kw)
    cmask = (col[None, :] >= cstart[:, None]) & (col[None, :] < cstart[:, None] + kw)
    cidx = jnp.clip(col[None, :] - col[:, None] + kw - 1, 0, 2 * kw - 2)
    rpb_c = rpb[:, :, cidx]
    kg = k.reshape(B, rows, GRID_W, H, d)
    vg = v.reshape(B, rows, GRID_W, H, d)
    qg = q.reshape(B, rows, GRID_W, H, d)

    def row_block(args):
        r, q_r = args
        start = jnp.clip(r - kh // 2, 0, rows - kh)
        k_b = lax.dynamic_slice_in_dim(kg, start, kh, axis=1)
        v_b = lax.dynamic_slice_in_dim(vg, start, kh, axis=1)
        ridx = start + jnp.arange(kh) - r + NA_KH - 1
        bias = jnp.transpose(rpb_c[:, ridx], (0, 2, 1, 3)).astype(jnp.float32)
        s_loc = jnp.einsum('bqhd,bikhd->bhqik', q_r, k_b).astype(jnp.float32) * scale + bias[None]
        s_loc = jnp.where(cmask[None, None, :, None, :], s_loc, NEG_INF).reshape(B, H, GRID_W, kh * GRID_W)
        s_ctx = jnp.einsum('bqhd,bchd->bhqc', q_r, kc).astype(jnp.float32) * scale
        p = jax.nn.softmax(jnp.concatenate([s_loc, s_ctx], axis=-1), axis=-1)
        p_loc = p[..., :kh * GRID_W].reshape(B, H, GRID_W, kh, GRID_W).astype(v.dtype)
        p_ctx = p[..., kh * GRID_W:].astype(v.dtype)
        return (jnp.einsum('bhqik,bikhd->bqhd', p_loc, v_b)
                + jnp.einsum('bhqc,bchd->bqhd', p_ctx, vc))

    out = lax.map(row_block, (jnp.arange(rows), jnp.moveaxis(qg, 1, 0)))
    return jnp.moveaxis(out, 0, 1).reshape(B, L, H, d)


def ctx_attention(q, k, v):
    s = jnp.einsum('bqhd,bkhd->bhqk', q, k).astype(jnp.float32) * q.shape[-1] ** -0.5
    p = jax.nn.softmax(s, axis=-1).astype(v.dtype)
    return jnp.einsum('bhqk,bkhd->bqhd', p, v)


def peer(h, w_q, sub_keys, exp_u, exp_v):
    B, L, D = h.shape
    tok = h.reshape(-1, PEER_BLOCK, D)
    K = PEER_TOPK

    def block(xb):
        T = xb.shape[0]
        q = (xb @ w_q).reshape(T, PEER_HEADS, 2, PEER_DK // 2)
        s = jnp.einsum('thpk,hpnk->thpn', q, sub_keys).astype(jnp.float32)
        top_s, top_i = lax.top_k(s, K)
        cand_s = (top_s[:, :, 0, :, None] + top_s[:, :, 1, None, :]).reshape(T, PEER_HEADS, K * K)
        cand_i = (top_i[:, :, 0, :, None] * PEER_NKEYS + top_i[:, :, 1, None, :]).reshape(T, PEER_HEADS, K * K)
        best_s, best_pos = lax.top_k(cand_s, K)
        e = jnp.take_along_axis(cand_i, best_pos, axis=-1)
        g = jax.nn.softmax(best_s, axis=-1).astype(xb.dtype)
        act = jax.nn.gelu(jnp.einsum('thkd,td->thk', exp_u[e], xb)) * g
        return jnp.einsum('thk,thkd->td', act, exp_v[e])

    return lax.map(block, tok).reshape(B, L, D)


def mixer(kv, rest, p, log_f, log_b, s_f, s_b, nk_c, nv_c, latent):
    B, L, _ = kv.shape
    hy = hyena_mixer(rest[..., R_HY:R_RQ], p['hy_conv_w'], p['hy_conv_b'], p['hy_f'], p['hy_skip'])
    rq = split_heads(rest[..., R_RQ:R_RG])
    rk = split_heads(kv[..., K_RK:K_RV]) * RET_SCALE
    rv = split_heads(kv[..., K_RV:K_NK])
    if latent:
        rq = axial_rope(rq)
        rk = axial_rope(rk)
    ro = retention_bidir(to_bhld(rq), to_bhld(rk), to_bhld(rv), log_f, log_b, s_f, s_b)
    ro = head_rms(jnp.swapaxes(ro, 1, 2)).reshape(B, L, RET_W).astype(kv.dtype)
    ro = ro * jax.nn.silu(rest[..., R_RG:R_NQ])
    nq = rmsnorm(split_heads(rest[..., R_NQ:]), p['q_norm'])
    if latent:
        nk = rmsnorm(split_heads(kv[..., K_NK:K_NV]), p['k_norm'])
        nv = split_heads(kv[..., K_NV:])
        no = na_latent(nq, nk, nv, nk_c, nv_c, p['na_rpb'])
    else:
        no = ctx_attention(nq, nk_c, nv_c)
    return jnp.concatenate([hy, ro, no.reshape(B, L, NA_W)], axis=-1) @ p['w_out']


def layer(xl, xc, mod_l, mod_c, p, last):
    sa_l, ca_l, ga_l, sf_l, cf_l, gf_l = jnp.split(mod_l, N_MOD, axis=-1)
    mod_c_parts = jnp.split(mod_c, mod_c.shape[-1] // D_MODEL, axis=-1)
    sa_c, ca_c = mod_c_parts[0], mod_c_parts[1]
    log_f = -jnp.exp(p['ret_decay'][0].astype(jnp.float32))
    log_b = -jnp.exp(p['ret_decay'][1].astype(jnp.float32))
    w_in = p['w_in']
    hc = modulate(rmsnorm(xc, p['norm_mix']), sa_c, ca_c)
    kv_c = hc @ w_in[:, :KV_COLS]
    rk_c = to_bhld(split_heads(kv_c[..., K_RK:K_RV]) * RET_SCALE)
    rv_c = to_bhld(split_heads(kv_c[..., K_RV:K_NK]))
    s_f = ret_final_state(rk_c, rv_c, log_f, False)
    s_b = ret_final_state(rk_c, rv_c, log_b, True)
    nk_c = rmsnorm(split_heads(kv_c[..., K_NK:K_NV]), p['k_norm'])
    nv_c = split_heads(kv_c[..., K_NV:])
    hl = modulate(rmsnorm(xl, p['norm_mix']), sa_l, ca_l)
    y_l = mixer(hl @ w_in[:, :KV_COLS], hl @ w_in[:, KV_COLS:], p, log_f, log_b, s_f, s_b, nk_c, nv_c, True)
    xl = xl + ga_l * y_l
    hf_l = modulate(rmsnorm(xl, p['norm_ffn']), sf_l, cf_l)
    xl = xl + gf_l * peer(hf_l, p['peer_wq'], p['peer_keys'], p['peer_u'], p['peer_v'])
    if last:
        return xl, None
    ga_c, sf_c, cf_c, gf_c = mod_c_parts[2], mod_c_parts[3], mod_c_parts[4], mod_c_parts[5]
    zeros = jnp.zeros_like(s_f)
    y_c = mixer(kv_c, hc @ w_in[:, KV_COLS:], p, log_f, log_b, zeros, zeros, nk_c, nv_c, False)
    xc = xc + ga_c * y_c
    hf_c = modulate(rmsnorm(xc, p['norm_ffn']), sf_c, cf_c)
    xc = xc + gf_c * peer(hf_c, p['peer_wq'], p['peer_keys'], p['peer_u'], p['peer_v'])
    return xl, xc


def setup_inputs(seed: int = 0) -> dict:
    key = jax.random.key(seed)
    ks = jax.random.split(key, 32)
    D = D_MODEL

    def nrm(k, shape, s):
        return jax.random.normal(k, shape, jnp.float32) * s

    a0 = jnp.asarray(np.log(-np.log1p(-(2.0 ** (-5.0 - np.arange(RET_HEADS))))), dtype=jnp.float32)
    return {
        'x': nrm(ks[0], (BATCH, SEQ, D), 1.0),
        'c': nrm(ks[1], (BATCH, D), 1.0),
        'ctx': nrm(ks[2], (BATCH, CTX_LEN, D), 1.0),
        'c_ctx': nrm(ks[3], (D,), 1.0),
        'w_mod': nrm(ks[4], (DEPTH, D, N_MOD * D), D ** -0.5),
        'b_mod': nrm(ks[5], (DEPTH, N_MOD * D), 0.02),
        'norm_mix': 1.0 + nrm(ks[6], (DEPTH, D), 0.02),
        'w_in': nrm(ks[7], (DEPTH, D, IN_COLS), D ** -0.5),
        'hy_conv_w': nrm(ks[8], (DEPTH, HY_SHORT, 3 * HY_W), HY_SHORT ** -0.5),
        'hy_conv_b': nrm(ks[9], (DEPTH, 3 * HY_W), 0.02),
        'hy_f_w1': nrm(ks[10], (DEPTH, HY_EMB, HY_FILTER_HIDDEN), HY_EMB ** -0.5),
        'hy_f_b1': nrm(ks[11], (DEPTH, HY_FILTER_HIDDEN), 0.1),
        'hy_f_freq': 1.0 + nrm(ks[12], (DEPTH, HY_FILTER_HIDDEN), 0.02),
        'hy_f_w2': nrm(ks[13], (DEPTH, HY_FILTER_HIDDEN, HY_FILTER_HIDDEN), HY_FILTER_HIDDEN ** -0.5),
        'hy_f_b2': nrm(ks[14], (DEPTH, HY_FILTER_HIDDEN), 0.1),
        'hy_f_w3': nrm(ks[15], (DEPTH, HY_FILTER_HIDDEN, HY_FILTER_HIDDEN), HY_FILTER_HIDDEN ** -0.5),
        'hy_f_b3': nrm(ks[16], (DEPTH, HY_FILTER_HIDDEN), 0.1),
        'hy_f_w4': nrm(ks[17], (DEPTH, HY_FILTER_HIDDEN, HY_ORDER * 2 * HY_W), HY_FILTER_HIDDEN ** -0.5 * HY_FILTER_SCALE),
        'hy_skip': nrm(ks[18], (DEPTH, HY_ORDER, HY_W), 0.5),
        'ret_decay': a0 + nrm(ks[19], (DEPTH, 2, RET_HEADS), 0.05),
        'na_rpb': nrm(ks[20], (DEPTH, NA_HEADS, 2 * NA_KH - 1, 2 * NA_KW - 1), 0.1),
        'q_norm': 1.0 + nrm(ks[21], (DEPTH, HEAD_DIM), 0.02),
        'k_norm': 1.0 + nrm(ks[22], (DEPTH, HEAD_DIM), 0.02),
        'w_out': nrm(ks[23], (DEPTH, D, D), D ** -0.5),
        'norm_ffn': 1.0 + nrm(ks[24], (DEPTH, D), 0.02),
        'peer_wq': nrm(ks[25], (DEPTH, D, PEER_HEADS * PEER_DK), D ** -0.5),
        'peer_keys': nrm(ks[26], (DEPTH, PEER_HEADS, 2, PEER_NKEYS, PEER_DK // 2), (PEER_DK // 2) ** -0.5),
        'peer_u': nrm(ks[27], (DEPTH, PEER_N, D), D ** -0.5),
        'peer_v': nrm(ks[28], (DEPTH, PEER_N, D), 1.0),
    }


def reference(x, c, ctx, c_ctx, w_mod, b_mod, norm_mix, w_in, hy_conv_w, hy_conv_b,
              hy_f_w1, hy_f_b1, hy_f_freq, hy_f_w2, hy_f_b2, hy_f_w3, hy_f_b3, hy_f_w4,
              hy_skip, ret_decay, na_rpb, q_norm, k_norm, w_out, norm_ffn,
              peer_wq, peer_keys, peer_u, peer_v):
    xl, xc = x, ctx
    sc = jax.nn.silu(c)
    scc = jax.nn.silu(c_ctx)
    for l in range(DEPTH):
        last = l == DEPTH - 1
        p = {
            'norm_mix': norm_mix[l], 'w_in': w_in[l],
            'hy_conv_w': hy_conv_w[l], 'hy_conv_b': hy_conv_b[l],
            'hy_f': (hy_f_w1[l], hy_f_b1[l], hy_f_freq[l], hy_f_w2[l], hy_f_b2[l],
                     hy_f_w3[l], hy_f_b3[l], hy_f_w4[l]),
            'hy_skip': hy_skip[l], 'ret_decay': ret_decay[l], 'na_rpb': na_rpb[l],
            'q_norm': q_norm[l], 'k_norm': k_norm[l], 'w_out': w_out[l],
            'norm_ffn': norm_ffn[l], 'peer_wq': peer_wq[l], 'peer_keys': peer_keys[l],
            'peer_u': peer_u[l], 'peer_v': peer_v[l],
        }
        mod_l = (sc @ w_mod[l] + b_mod[l])[:, None, :]
        n_ctx = 2 if last else N_MOD
        mod_c = (scc @ w_mod[l][:, :n_ctx * D_MODEL] + b_mod[l][:n_ctx * D_MODEL])[None, None, :]
        xl, xc = layer(xl, xc, mod_l, mod_c, p, last)
    return xl
```

```python
import math

import jax
import jax.numpy as jnp
from jax import lax
from jax.experimental import pallas as pl
from jax.experimental.pallas import tpu as pltpu

D_MODEL = 4096
BATCH = 4
SEQ = 4096
DEPTH = 2

GRID_W = 64
CTX_LEN = 256
HEAD_DIM = 128
N_MIX_HEADS = D_MODEL // HEAD_DIM
HY_GROUPS = N_MIX_HEADS // 4
RET_HEADS = (N_MIX_HEADS - HY_GROUPS) // 2
NA_HEADS = N_MIX_HEADS - HY_GROUPS - RET_HEADS
HY_W = HY_GROUPS * HEAD_DIM
RET_W = RET_HEADS * HEAD_DIM
NA_W = NA_HEADS * HEAD_DIM
K_RK = 0
K_RV = K_RK + RET_W
K_NK = K_RV + RET_W
K_NV = K_NK + NA_W
KV_COLS = K_NV + NA_W
R_HY = 0
R_RQ = R_HY + 3 * HY_W
R_RG = R_RQ + RET_W
R_NQ = R_RG + RET_W
REST_COLS = R_NQ + NA_W
IN_COLS = KV_COLS + REST_COLS
N_MOD = 6
HY_ORDER = 2
HY_SHORT = 3
HY_EMB = 33
HY_BANDS = (HY_EMB - 1) // 2
HY_FILTER_HIDDEN = 64
HY_DECAY_TARGET = 1e-2
HY_SHORT_DECAY_PCT = 0.3
HY_LONG_DECAY_PCT = 1.5
HY_MAX_DECAY = math.log(HY_DECAY_TARGET) / HY_SHORT_DECAY_PCT
HY_MIN_DECAY = math.log(HY_DECAY_TARGET) / HY_LONG_DECAY_PCT
HY_FILTER_SCALE = 0.05
RET_CHUNK = 128
RET_SCALE = HEAD_DIM ** -0.5
ROPE_BASE = 10000.0
NA_KH = 8
NA_KW = 16
PEER_HEADS = 8
PEER_NKEYS = 128
PEER_N = PEER_NKEYS * PEER_NKEYS
PEER_TOPK = 16
PEER_DK = 256
PEER_BLOCK = 64
EPS = 1e-6
NEG_INF = -1e30

VMEM_LIMIT_BYTES = 56 * 1024 * 1024


def _mm_kernel(a_ref, b_ref, o_ref):
    o_ref[...] = jnp.dot(a_ref[...], b_ref[...], preferred_element_type=jnp.float32)


def _pick_block(n, candidates):
    for c in candidates:
        if n % c == 0:
            return c
    return n


def mm(a, b):
    lead = a.shape[:-1]
    K = a.shape[-1]
    a2 = a.reshape(-1, K).astype(jnp.bfloat16)
    b2 = b.astype(jnp.bfloat16)
    M, N = a2.shape[0], b2.shape[1]
    bm = _pick_block(M, (1024, 512, 256, 128, 64, 32, 16, 8))
    bn = _pick_block(N, (1024, 768, 512, 256, 128))
    out = pl.pallas_call(
        _mm_kernel,
        grid=(M // bm, N // bn),
        in_specs=[pl.BlockSpec((bm, K), lambda i, j: (i, 0)),
                  pl.BlockSpec((K, bn), lambda i, j: (0, j))],
        out_specs=pl.BlockSpec((bm, bn), lambda i, j: (i, j)),
        out_shape=jax.ShapeDtypeStruct((M, N), jnp.float32),
        compiler_params=pltpu.CompilerParams(
            dimension_semantics=("parallel", "parallel"),
            vmem_limit_bytes=VMEM_LIMIT_BYTES),
        name="mm",
    )(a2, b2)
    return out.reshape(lead + (N,))


def rmsnorm(x, g):
    xf = x.astype(jnp.float32)
    y = xf * lax.rsqrt(jnp.mean(xf * xf, axis=-1, keepdims=True) + EPS)
    return (y * g.astype(jnp.float32)).astype(x.dtype)


def head_rms(x):
    xf = x.astype(jnp.float32)
    return xf * lax.rsqrt(jnp.mean(xf * xf, axis=-1, keepdims=True) + EPS)


def modulate(h, shift, scale):
    return h * (1.0 + scale) + shift


def split_heads(x):
    return x.reshape(x.shape[:-1] + (x.shape[-1] // HEAD_DIM, HEAD_DIM))


def to_bhld(x):
    return jnp.swapaxes(x, 1, 2).astype(jnp.float32)


def axial_rope(x):
    L = x.shape[1]
    t = jnp.arange(L)
    row = (t // GRID_W).astype(jnp.float32)
    col = (t % GRID_W).astype(jnp.float32)
    quarter = x.shape[-1] // 4
    inv = ROPE_BASE ** (-jnp.arange(quarter, dtype=jnp.float32) / quarter)

    def rot(xp, pos):
        ang = pos[:, None] * inv[None, :]
        cos = jnp.cos(ang)[None, :, None, :]
        sin = jnp.sin(ang)[None, :, None, :]
        a, b = xp[..., :quarter], xp[..., quarter:]
        return jnp.concatenate([a * cos - b * sin, a * sin + b * cos], axis=-1)

    half = 2 * quarter
    return jnp.concatenate([rot(x[..., :half], row), rot(x[..., half:], col)], axis=-1).astype(x.dtype)


def short_conv(u, w, b):
    C = u.shape[-1]
    y = lax.conv_general_dilated(u, w[:, None, :].astype(u.dtype), window_strides=(1,),
                                 padding=[(HY_SHORT // 2, HY_SHORT // 2)],
                                 dimension_numbers=('NWC', 'WIO', 'NWC'),
                                 feature_group_count=C)
    return y + b.astype(u.dtype)


def hyena_filters(L, w1, b1, freq, w2, b2, w3, b3, w4):
    f32 = lambda a: a.astype(jnp.float32)
    t = jnp.linspace(0.0, 1.0, L, dtype=jnp.float32)[:, None]
    w = 2.0 * math.pi * jnp.arange(L, dtype=jnp.float32)[:, None] / L
    f = jnp.linspace(1e-4, HY_BANDS - 1, HY_BANDS, dtype=jnp.float32)[None, :]
    z = jnp.concatenate([t, jnp.cos(f * w), -jnp.sin(f * w)], axis=-1)
    h = jnp.sin(f32(freq) * (z @ f32(w1) + f32(b1)))
    h = jnp.sin(f32(freq) * (h @ f32(w2) + f32(b2)))
    h = jnp.sin(f32(freq) * (h @ f32(w3) + f32(b3)))
    k = (h @ f32(w4)).reshape(L, HY_ORDER, 2, HY_W)
    deltas = jnp.abs(jnp.linspace(HY_MIN_DECAY, HY_MAX_DECAY, HY_W, dtype=jnp.float32))
    return k * jnp.exp(-t[:, :, None, None] * deltas)


def long_conv_bidir(u, kf, kb, skip):
    L, C = u.shape[1], u.shape[2]
    filt = jnp.concatenate([kf, jnp.zeros((1, C), jnp.float32), kb[:0:-1]], axis=0)
    uf = u.astype(jnp.float32)
    U = jnp.fft.rfft(uf, n=2 * L, axis=1)
    K = jnp.fft.rfft(filt, n=2 * L, axis=0)
    y = jnp.fft.irfft(U * K[None], n=2 * L, axis=1)[:, :L]
    return (y + uf * skip.astype(jnp.float32)).astype(u.dtype)


def hyena_mixer(u, conv_w, conv_b, filt, skip):
    L = u.shape[1]
    u = short_conv(u, conv_w, conv_b)
    v, x1, x2 = jnp.split(u, 3, axis=-1)
    k = hyena_filters(L, *filt)
    z = x1 * long_conv_bidir(v, k[:, 0, 0], k[:, 0, 1], skip[0])
    z = x2 * long_conv_bidir(z, k[:, 1, 0], k[:, 1, 1], skip[1])
    return z


def ret_chunkwise(q, k, v, log_g, s0):
    B, H, L, d = q.shape
    C = RET_CHUNK
    N = L // C
    pos = jnp.arange(C, dtype=jnp.float32)
    diff = pos[:, None] - pos[None, :]
    inner_decay = jnp.where(diff >= 0, jnp.exp(log_g[:, None, None] * jnp.maximum(diff, 0.0)), 0.0)
    q_decay = jnp.exp(log_g[:, None] * (pos + 1.0))[:, :, None]
    k_decay = jnp.exp(log_g[:, None] * (C - 1.0 - pos))[:, :, None]
    chunk_decay = jnp.exp(log_g * C)[:, None, None]

    def chunks(a):
        return jnp.moveaxis(a.reshape(B, H, N, C, d), 2, 0)

    def step(S, qkv):
        qn, kn, vn = qkv
        a = jnp.einsum('bhid,bhjd->bhij', qn, kn) * inner_decay
        o = jnp.einsum('bhij,bhje->bhie', a, vn) + jnp.einsum('bhid,bhde->bhie', qn * q_decay, S)
        S = S * chunk_decay + jnp.einsum('bhjd,bhje->bhde', kn * k_decay, vn)
        return S, o

    _, o = lax.scan(step, s0, (chunks(q), chunks(k), chunks(v)))
    return jnp.moveaxis(o, 0, 2).reshape(B, H, L, d)


def ret_final_state(k, v, log_g, reverse):
    L = k.shape[2]
    pos = jnp.arange(L, dtype=jnp.float32)
    expo = pos if reverse else (L - 1.0 - pos)
    w = jnp.exp(log_g[:, None] * expo)
    return jnp.einsum('bhld,bhle->bhde', k * w[None, :, :, None], v)


def retention_bidir(q, k, v, log_f, log_b, s_f, s_b):
    fwd = ret_chunkwise(q, k, v, log_f, s_f)
    bwd = jnp.flip(ret_chunkwise(jnp.flip(q, 2), jnp.flip(k, 2), jnp.flip(v, 2), log_b, s_b), 2)
    return fwd + bwd


def na_latent(q, k, v, kc, vc, rpb):
    B, L, H, d = q.shape
    rows = L // GRID_W
    kh = min(NA_KH, rows)
    kw = NA_KW
    scale = d ** -0.5
    col = jnp.arange(GRID_W)
    cstart = jnp.clip(col - kw // 2, 0, GRID_W - kw)
    cmask = (col[None, :] >= cstart[:, None]) & (col[None, :] < cstart[:, None] + kw)
    cidx = jnp.clip(col[None, :] - col[:, None] + kw - 1, 0, 2 * kw - 2)
    rpb_c = rpb[:, :, cidx]
    kg = k.reshape(B, rows, GRID_W, H, d)
    vg = v.reshape(B, rows, GRID_W, H, d)
    qg = q.reshape(B, rows, GRID_W, H, d)

    def row_block(args):
        r, q_r = args
        start = jnp.clip(r - kh // 2, 0, rows - kh)
        k_b = lax.dynamic_slice_in_dim(kg, start, kh, axis=1)
        v_b = lax.dynamic_slice_in_dim(vg, start, kh, axis=1)
        ridx = start + jnp.arange(kh) - r + NA_KH - 1
        bias = jnp.transpose(rpb_c[:, ridx], (0, 2, 1, 3)).astype(jnp.float32)
        s_loc = jnp.einsum('bqhd,bikhd->bhqik', q_r, k_b).astype(jnp.float32) * scale + bias[None]
        s_loc = jnp.where(cmask[None, None, :, None, :], s_loc, NEG_INF).reshape(B, H, GRID_W, kh * GRID_W)
        s_ctx = jnp.einsum('bqhd,bchd->bhqc', q_r, kc).astype(jnp.float32) * scale
        p = jax.nn.softmax(jnp.concatenate([s_loc, s_ctx], axis=-1), axis=-1)
        p_loc = p[..., :kh * GRID_W].reshape(B, H, GRID_W, kh, GRID_W).astype(v.dtype)
        p_ctx = p[..., kh * GRID_W:].astype(v.dtype)
        return (jnp.einsum('bhqik,bikhd->bqhd', p_loc, v_b)
                + jnp.einsum('bhqc,bchd->bqhd', p_ctx, vc))

    out = lax.map(row_block, (jnp.arange(rows), jnp.moveaxis(qg, 1, 0)))
    return jnp.moveaxis(out, 0, 1).reshape(B, L, H, d)


def ctx_attention(q, k, v):
    s = jnp.einsum('bqhd,bkhd->bhqk', q, k).astype(jnp.float32) * q.shape[-1] ** -0.5
    p = jax.nn.softmax(s, axis=-1).astype(v.dtype)
    return jnp.einsum('bhqk,bkhd->bqhd', p, v)


def peer(h, w_q, sub_keys, exp_u, exp_v):
    B, L, D = h.shape
    tok = h.reshape(-1, PEER_BLOCK, D)
    q_all = mm(h, w_q).reshape(-1, PEER_BLOCK, PEER_HEADS, 2, PEER_DK // 2)
    K = PEER_TOPK

    def block(args):
        xb, q = args
        T = xb.shape[0]
        s = jnp.einsum('thpk,hpnk->thpn', q, sub_keys).astype(jnp.float32)
        top_s, top_i = lax.top_k(s, K)
        cand_s = (top_s[:, :, 0, :, None] + top_s[:, :, 1, None, :]).reshape(T, PEER_HEADS, K * K)
        cand_i = (top_i[:, :, 0, :, None] * PEER_NKEYS + top_i[:, :, 1, None, :]).reshape(T, PEER_HEADS, K * K)
        best_s, best_pos = lax.top_k(cand_s, K)
        e = jnp.take_along_axis(cand_i, best_pos, axis=-1)
        g = jax.nn.softmax(best_s, axis=-1).astype(xb.dtype)
        act = jax.nn.gelu(jnp.einsum('thkd,td->thk', exp_u[e], xb)) * g
        return jnp.einsum('thk,thkd->td', act, exp_v[e])

    return lax.map(block, (tok, q_all)).reshape(B, L, D)


def mixer(kv, rest, p, log_f, log_b, s_f, s_b, nk_c, nv_c, latent):
    B, L, _ = kv.shape
    hy = hyena_mixer(rest[..., R_HY:R_RQ], p['hy_conv_w'], p['hy_conv_b'], p['hy_f'], p['hy_skip'])
    rq = split_heads(rest[..., R_RQ:R_RG])
    rk = split_heads(kv[..., K_RK:K_RV]) * RET_SCALE
    rv = split_heads(kv[..., K_RV:K_NK])
    if latent:
        rq = axial_rope(rq)
        rk = axial_rope(rk)
    ro = retention_bidir(to_bhld(rq), to_bhld(rk), to_bhld(rv), log_f, log_b, s_f, s_b)
    ro = head_rms(jnp.swapaxes(ro, 1, 2)).reshape(B, L, RET_W).astype(kv.dtype)
    ro = ro * jax.nn.silu(rest[..., R_RG:R_NQ])
    nq = rmsnorm(split_heads(rest[..., R_NQ:]), p['q_norm'])
    if latent:
        nk = rmsnorm(split_heads(kv[..., K_NK:K_NV]), p['k_norm'])
        nv = split_heads(kv[..., K_NV:])
        no = na_latent(nq, nk, nv, nk_c, nv_c, p['na_rpb'])
    else:
        no = ctx_attention(nq, nk_c, nv_c)
    return mm(jnp.concatenate([hy, ro, no.reshape(B, L, NA_W)], axis=-1), p['w_out'])


def layer(xl, xc, mod_l, mod_c, p, last):
    sa_l, ca_l, ga_l, sf_l, cf_l, gf_l = jnp.split(mod_l, N_MOD, axis=-1)
    mod_c_parts = jnp.split(mod_c, mod_c.shape[-1] // D_MODEL, axis=-1)
    sa_c, ca_c = mod_c_parts[0], mod_c_parts[1]
    log_f = -jnp.exp(p['ret_decay'][0].astype(jnp.float32))
    log_b = -jnp.exp(p['ret_decay'][1].astype(jnp.float32))
    w_in = p['w_in']
    hc = modulate(rmsnorm(xc, p['norm_mix']), sa_c, ca_c)
    kv_c = mm(hc, w_in[:, :KV_COLS])
    rk_c = to_bhld(split_heads(kv_c[..., K_RK:K_RV]) * RET_SCALE)
    rv_c = to_bhld(split_heads(kv_c[..., K_RV:K_NK]))
    s_f = ret_final_state(rk_c, rv_c, log_f, False)
    s_b = ret_final_state(rk_c, rv_c, log_b, True)
    nk_c = rmsnorm(split_heads(kv_c[..., K_NK:K_NV]), p['k_norm'])
    nv_c = split_heads(kv_c[..., K_NV:])
    hl = modulate(rmsnorm(xl, p['norm_mix']), sa_l, ca_l)
    y_l = mixer(mm(hl, w_in[:, :KV_COLS]), mm(hl, w_in[:, KV_COLS:]), p, log_f, log_b, s_f, s_b, nk_c, nv_c, True)
    xl = xl + ga_l * y_l
    hf_l = modulate(rmsnorm(xl, p['norm_ffn']), sf_l, cf_l)
    xl = xl + gf_l * peer(hf_l, p['peer_wq'], p['peer_keys'], p['peer_u'], p['peer_v'])
    if last:
        return xl, None
    ga_c, sf_c, cf_c, gf_c = mod_c_parts[2], mod_c_parts[3], mod_c_parts[4], mod_c_parts[5]
    zeros = jnp.zeros_like(s_f)
    y_c = mixer(kv_c, mm(hc, w_in[:, KV_COLS:]), p, log_f, log_b, zeros, zeros, nk_c, nv_c, False)
    xc = xc + ga_c * y_c
    hf_c = modulate(rmsnorm(xc, p['norm_ffn']), sf_c, cf_c)
    xc = xc + gf_c * peer(hf_c, p['peer_wq'], p['peer_keys'], p['peer_u'], p['peer_v'])
    return xl, xc


def kernel(x, c, ctx, c_ctx, w_mod, b_mod, norm_mix, w_in, hy_conv_w, hy_conv_b,
           hy_f_w1, hy_f_b1, hy_f_freq, hy_f_w2, hy_f_b2, hy_f_w3, hy_f_b3, hy_f_w4,
           hy_skip, ret_decay, na_rpb, q_norm, k_norm, w_out, norm_ffn,
           peer_wq, peer_keys, peer_u, peer_v):
    xl, xc = x, ctx
    sc = jax.nn.silu(c)
    scc = jax.nn.silu(c_ctx)
    for l in range(DEPTH):
        last = l == DEPTH - 1
        p = {
            'norm_mix': norm_mix[l], 'w_in': w_in[l],
            'hy_conv_w': hy_conv_w[l], 'hy_conv_b': hy_conv_b[l],
            'hy_f': (hy_f_w1[l], hy_f_b1[l], hy_f_freq[l], hy_f_w2[l], hy_f_b2[l],
                     hy_f_w3[l], hy_f_b3[l], hy_f_w4[l]),
            'hy_skip': hy_skip[l], 'ret_decay': ret_decay[l], 'na_rpb': na_rpb[l],
            'q_norm': q_norm[l], 'k_norm': k_norm[l], 'w_out': w_out[l],
            'norm_ffn': norm_ffn[l], 'peer_wq': peer_wq[l], 'peer_keys': peer_keys[l],
            'peer_u': peer_u[l], 'peer_v': peer_v[l],
        }
        mod_l = (sc @ w_mod[l] + b_mod[l])[:, None, :]
        n_ctx = 2 if last else N_MOD
        mod_c = (scc @ w_mod[l][:, :n_ctx * D_MODEL] + b_mod[l][:n_ctx * D_MODEL])[None, None, :]
        xl, xc = layer(xl, xc, mod_l, mod_c, p, last)
    return xl
```

```python
import functools
import math

import jax
import jax.numpy as jnp
from jax import lax
from jax.experimental import pallas as pl
from jax.experimental.pallas import tpu as pltpu

D_MODEL = 4096
BATCH = 4
SEQ = 4096
DEPTH = 2

GRID_W = 64
CTX_LEN = 256
HEAD_DIM = 128
N_MIX_HEADS = D_MODEL // HEAD_DIM
HY_GROUPS = N_MIX_HEADS // 4
RET_HEADS = (N_MIX_HEADS - HY_GROUPS) // 2
NA_HEADS = N_MIX_HEADS - HY_GROUPS - RET_HEADS
HY_W = HY_GROUPS * HEAD_DIM
RET_W = RET_HEADS * HEAD_DIM
NA_W = NA_HEADS * HEAD_DIM
K_RK = 0
K_RV = K_RK + RET_W
K_NK = K_RV + RET_W
K_NV = K_NK + NA_W
KV_COLS = K_NV + NA_W
R_HY = 0
R_RQ = R_HY + 3 * HY_W
R_RG = R_RQ + RET_W
R_NQ = R_RG + RET_W
REST_COLS = R_NQ + NA_W
IN_COLS = KV_COLS + REST_COLS
N_MOD = 6
HY_ORDER = 2
HY_SHORT = 3
HY_EMB = 33
HY_BANDS = (HY_EMB - 1) // 2
HY_FILTER_HIDDEN = 64
HY_DECAY_TARGET = 1e-2
HY_SHORT_DECAY_PCT = 0.3
HY_LONG_DECAY_PCT = 1.5
HY_MAX_DECAY = math.log(HY_DECAY_TARGET) / HY_SHORT_DECAY_PCT
HY_MIN_DECAY = math.log(HY_DECAY_TARGET) / HY_LONG_DECAY_PCT
HY_FILTER_SCALE = 0.05
RET_CHUNK = 128
RET_SCALE = HEAD_DIM ** -0.5
ROPE_BASE = 10000.0
NA_KH = 8
NA_KW = 16
PEER_HEADS = 8
PEER_NKEYS = 128
PEER_N = PEER_NKEYS * PEER_NKEYS
PEER_TOPK = 16
PEER_DK = 256
PEER_BLOCK = 64
EPS = 1e-6
NEG_INF = -1e30

VMEM_LIMIT_BYTES = 56 * 1024 * 1024


def _mm_kernel(a_ref, b_ref, o_ref):
    o_ref[...] = jnp.dot(a_ref[...], b_ref[...], preferred_element_type=jnp.float32)


def _pick_block(n, candidates):
    for c in candidates:
        if n % c == 0:
            return c
    return n


def mm(a, b):
    lead = a.shape[:-1]
    K = a.shape[-1]
    a2 = a.reshape(-1, K).astype(jnp.bfloat16)
    b2 = b.astype(jnp.bfloat16)
    M, N = a2.shape[0], b2.shape[1]
    bm = _pick_block(M, (1024, 512, 256, 128, 64, 32, 16, 8))
    bn = _pick_block(N, (1024, 768, 512, 256, 128))
    out = pl.pallas_call(
        _mm_kernel,
        grid=(M // bm, N // bn),
        in_specs=[pl.BlockSpec((bm, K), lambda i, j: (i, 0)),
                  pl.BlockSpec((K, bn), lambda i, j: (0, j))],
        out_specs=pl.BlockSpec((bm, bn), lambda i, j: (i, j)),
        out_shape=jax.ShapeDtypeStruct((M, N), jnp.float32),
        compiler_params=pltpu.CompilerParams(
            dimension_semantics=("parallel", "parallel"),
            vmem_limit_bytes=VMEM_LIMIT_BYTES),
        name="mm",
    )(a2, b2)
    return out.reshape(lead + (N,))


def rmsnorm(x, g):
    xf = x.astype(jnp.float32)
    y = xf * lax.rsqrt(jnp.mean(xf * xf, axis=-1, keepdims=True) + EPS)
    return (y * g.astype(jnp.float32)).astype(x.dtype)


def head_rms(x):
    xf = x.astype(jnp.float32)
    return xf * lax.rsqrt(jnp.mean(xf * xf, axis=-1, keepdims=True) + EPS)


def modulate(h, shift, scale):
    return h * (1.0 + scale) + shift


def split_heads(x):
    return x.reshape(x.shape[:-1] + (x.shape[-1] // HEAD_DIM, HEAD_DIM))


def to_bhld(x):
    return jnp.swapaxes(x, 1, 2).astype(jnp.float32)


def axial_rope(x):
    L = x.shape[1]
    t = jnp.arange(L)
    row = (t // GRID_W).astype(jnp.float32)
    col = (t % GRID_W).astype(jnp.float32)
    quarter = x.shape[-1] // 4
    inv = ROPE_BASE ** (-jnp.arange(quarter, dtype=jnp.float32) / quarter)

    def rot(xp, pos):
        ang = pos[:, None] * inv[None, :]
        cos = jnp.cos(ang)[None, :, None, :]
        sin = jnp.sin(ang)[None, :, None, :]
        a, b = xp[..., :quarter], xp[..., quarter:]
        return jnp.concatenate([a * cos - b * sin, a * sin + b * cos], axis=-1)

    half = 2 * quarter
    return jnp.concatenate([rot(x[..., :half], row), rot(x[..., half:], col)], axis=-1).astype(x.dtype)


def short_conv(u, w, b):
    C = u.shape[-1]
    y = lax.conv_general_dilated(u, w[:, None, :].astype(u.dtype), window_strides=(1,),
                                 padding=[(HY_SHORT // 2, HY_SHORT // 2)],
                                 dimension_numbers=('NWC', 'WIO', 'NWC'),
                                 feature_group_count=C)
    return y + b.astype(u.dtype)


def hyena_filters(L, w1, b1, freq, w2, b2, w3, b3, w4):
    f32 = lambda a: a.astype(jnp.float32)
    t = jnp.linspace(0.0, 1.0, L, dtype=jnp.float32)[:, None]
    w = 2.0 * math.pi * jnp.arange(L, dtype=jnp.float32)[:, None] / L
    f = jnp.linspace(1e-4, HY_BANDS - 1, HY_BANDS, dtype=jnp.float32)[None, :]
    z = jnp.concatenate([t, jnp.cos(f * w), -jnp.sin(f * w)], axis=-1)
    h = jnp.sin(f32(freq) * (z @ f32(w1) + f32(b1)))
    h = jnp.sin(f32(freq) * (h @ f32(w2) + f32(b2)))
    h = jnp.sin(f32(freq) * (h @ f32(w3) + f32(b3)))
    k = (h @ f32(w4)).reshape(L, HY_ORDER, 2, HY_W)
    deltas = jnp.abs(jnp.linspace(HY_MIN_DECAY, HY_MAX_DECAY, HY_W, dtype=jnp.float32))
    return k * jnp.exp(-t[:, :, None, None] * deltas)


def long_conv_bidir(u, kf, kb, skip):
    L, C = u.shape[1], u.shape[2]
    filt = jnp.concatenate([kf, jnp.zeros((1, C), jnp.float32), kb[:0:-1]], axis=0)
    uf = u.astype(jnp.float32)
    U = jnp.fft.rfft(uf, n=2 * L, axis=1)
    K = jnp.fft.rfft(filt, n=2 * L, axis=0)
    y = jnp.fft.irfft(U * K[None], n=2 * L, axis=1)[:, :L]
    return (y + uf * skip.astype(jnp.float32)).astype(u.dtype)


def hyena_mixer(u, conv_w, conv_b, filt, skip):
    L = u.shape[1]
    u = short_conv(u, conv_w, conv_b)
    v, x1, x2 = jnp.split(u, 3, axis=-1)
    k = hyena_filters(L, *filt)
    z = x1 * long_conv_bidir(v, k[:, 0, 0], k[:, 0, 1], skip[0])
    z = x2 * long_conv_bidir(z, k[:, 1, 0], k[:, 1, 1], skip[1])
    return z


def ret_chunkwise(q, k, v, log_g, s0):
    B, H, L, d = q.shape
    C = RET_CHUNK
    N = L // C
    pos = jnp.arange(C, dtype=jnp.float32)
    diff = pos[:, None] - pos[None, :]
    inner_decay = jnp.where(diff >= 0, jnp.exp(log_g[:, None, None] * jnp.maximum(diff, 0.0)), 0.0)
    q_decay = jnp.exp(log_g[:, None] * (pos + 1.0))[:, :, None]
    k_decay = jnp.exp(log_g[:, None] * (C - 1.0 - pos))[:, :, None]
    chunk_decay = jnp.exp(log_g * C)[:, None, None]

    def chunks(a):
        return jnp.moveaxis(a.reshape(B, H, N, C, d), 2, 0)

    def step(S, qkv):
        qn, kn, vn = qkv
        a = jnp.einsum('bhid,bhjd->bhij', qn, kn) * inner_decay
        o = jnp.einsum('bhij,bhje->bhie', a, vn) + jnp.einsum('bhid,bhde->bhie', qn * q_decay, S)
        S = S * chunk_decay + jnp.einsum('bhjd,bhje->bhde', kn * k_decay, vn)
        return S, o

    _, o = lax.scan(step, s0, (chunks(q), chunks(k), chunks(v)))
    return jnp.moveaxis(o, 0, 2).reshape(B, H, L, d)


def ret_final_state(k, v, log_g, reverse):
    L = k.shape[2]
    pos = jnp.arange(L, dtype=jnp.float32)
    expo = pos if reverse else (L - 1.0 - pos)
    w = jnp.exp(log_g[:, None] * expo)
    return jnp.einsum('bhld,bhle->bhde', k * w[None, :, :, None], v)


def retention_bidir(q, k, v, log_f, log_b, s_f, s_b):
    fwd = ret_chunkwise(q, k, v, log_f, s_f)
    bwd = jnp.flip(ret_chunkwise(jnp.flip(q, 2), jnp.flip(k, 2), jnp.flip(v, 2), log_b, s_b), 2)
    return fwd + bwd


def na_latent(q, k, v, kc, vc, rpb):
    B, L, H, d = q.shape
    rows = L // GRID_W
    kh = min(NA_KH, rows)
    kw = NA_KW
    scale = d ** -0.5
    col = jnp.arange(GRID_W)
    cstart = jnp.clip(col - kw // 2, 0, GRID_W - kw)
    cmask = (col[None, :] >= cstart[:, None]) & (col[None, :] < cstart[:, None] + kw)
    cidx = jnp.clip(col[None, :] - col[:, None] + kw - 1, 0, 2 * kw - 2)
    rpb_c = rpb[:, :, cidx]
    kg = k.reshape(B, rows, GRID_W, H, d)
    vg = v.reshape(B, rows, GRID_W, H, d)
    qg = q.reshape(B, rows, GRID_W, H, d)

    def row_block(args):
        r, q_r = args
        start = jnp.clip(r - kh // 2, 0, rows - kh)
        k_b = lax.dynamic_slice_in_dim(kg, start, kh, axis=1)
        v_b = lax.dynamic_slice_in_dim(vg, start, kh, axis=1)
        ridx = start + jnp.arange(kh) - r + NA_KH - 1
        bias = jnp.transpose(rpb_c[:, ridx], (0, 2, 1, 3)).astype(jnp.float32)
        s_loc = jnp.einsum('bqhd,bikhd->bhqik', q_r, k_b).astype(jnp.float32) * scale + bias[None]
        s_loc = jnp.where(cmask[None, None, :, None, :], s_loc, NEG_INF).reshape(B, H, GRID_W, kh * GRID_W)
        s_ctx = jnp.einsum('bqhd,bchd->bhqc', q_r, kc).astype(jnp.float32) * scale
        p = jax.nn.softmax(jnp.concatenate([s_loc, s_ctx], axis=-1), axis=-1)
        p_loc = p[..., :kh * GRID_W].reshape(B, H, GRID_W, kh, GRID_W).astype(v.dtype)
        p_ctx = p[..., kh * GRID_W:].astype(v.dtype)
        return (jnp.einsum('bhqik,bikhd->bqhd', p_loc, v_b)
                + jnp.einsum('bhqc,bchd->bqhd', p_ctx, vc))

    out = lax.map(row_block, (jnp.arange(rows), jnp.moveaxis(qg, 1, 0)))
    return jnp.moveaxis(out, 0, 1).reshape(B, L, H, d)


def ctx_attention(q, k, v):
    s = jnp.einsum('bqhd,bkhd->bhqk', q, k).astype(jnp.float32) * q.shape[-1] ** -0.5
    p = jax.nn.softmax(s, axis=-1).astype(v.dtype)
    return jnp.einsum('bhqk,bkhd->bqhd', p, v)


def _mm_nt_kernel(w_ref, x_ref, o_ref):
    o_ref[...] = lax.dot_general(w_ref[...], x_ref[...], (((1,), (1,)), ((), ())),
                                 preferred_element_type=jnp.float32)


def mm_nt(w, x):
    N, K = w.shape
    M = x.shape[0]
    bn = _pick_block(N, (1024, 512, 256, 128))
    bm = _pick_block(M, (1024, 512, 256, 128))
    return pl.pallas_call(
        _mm_nt_kernel,
        grid=(M // bm, N // bn),
        in_specs=[pl.BlockSpec((bn, K), lambda i, j: (j, 0)),
                  pl.BlockSpec((bm, K), lambda i, j: (i, 0))],
        out_specs=pl.BlockSpec((bn, bm), lambda i, j: (j, i)),
        out_shape=jax.ShapeDtypeStruct((N, M), jnp.float32),
        compiler_params=pltpu.CompilerParams(
            dimension_semantics=("parallel", "parallel"),
            vmem_limit_bytes=VMEM_LIMIT_BYTES),
        name="mm_nt",
    )(w.astype(jnp.bfloat16), x.astype(jnp.bfloat16))


def _top_values(s, k):
    vals = []
    work = s
    for _ in range(k):
        m = jnp.max(work, axis=0, keepdims=True)
        vals.append(m)
        work = jnp.where(work == m, -jnp.inf, work)
    return jnp.concatenate(vals, axis=0)


def _peer_route_kernel(qt_ref, keys_ref, s1_ref, b_ref, thr_ref, a_ref):
    half = PEER_DK // 2
    K = PEER_TOPK
    for h in range(PEER_HEADS):
        s = []
        for p in range(2):
            q = qt_ref[pl.ds((2 * h + p) * half, half), :]
            s.append(jnp.dot(keys_ref[2 * h + p], q, preferred_element_type=jnp.float32,
                             precision=lax.Precision.HIGHEST))
        s0, s1 = s
        v0 = _top_values(s0, K)
        v1 = _top_values(s1, K)
        cand = jnp.concatenate([v0[a:a + 1] + v1 for a in range(K)], axis=0)
        tau = _top_values(cand, K)[K - 1:K]
        m = v0[0:1] + v1[0:1]
        z = jnp.sum(jnp.where(cand >= tau, jnp.exp(cand - m), 0.0), axis=0, keepdims=True)
        thr = jnp.full_like(s0, jnp.inf)
        for b in range(K):
            vb = v1[b:b + 1]
            thr = jnp.minimum(thr, jnp.where(s0 + vb >= tau, vb, jnp.inf))
        s1_ref[h] = s1
        b_ref[h] = jnp.exp(s1 - v1[0:1])
        thr_ref[h] = thr
        a_ref[h] = jnp.exp(s0 - v0[0:1]) / z


PEER_ROUTE_LANES = 128


def peer_route(qt, sub_keys):
    T = qt.shape[1]
    keys = sub_keys.reshape(PEER_HEADS * 2, PEER_NKEYS, PEER_DK // 2).astype(jnp.float32)
    bt = PEER_ROUTE_LANES
    out = jax.ShapeDtypeStruct((PEER_HEADS, PEER_NKEYS, T), jnp.float32)
    ospec = pl.BlockSpec((PEER_HEADS, PEER_NKEYS, bt), lambda i: (0, 0, i))
    return pl.pallas_call(
        _peer_route_kernel,
        grid=(T // bt,),
        in_specs=[pl.BlockSpec((PEER_HEADS * PEER_DK, bt), lambda i: (0, i)),
                  pl.BlockSpec(keys.shape, lambda i: (0, 0, 0))],
        out_specs=[ospec] * 4,
        out_shape=[out] * 4,
        compiler_params=pltpu.CompilerParams(
            dimension_semantics=("parallel",), vmem_limit_bytes=VMEM_LIMIT_BYTES),
        name="peer_route",
    )(qt, keys)


def _peer_dense_kernel(x_ref, u_ref, v_ref, s1_ref, b_ref, thr_ref, a_ref, o_ref, *, groups):
    e = pl.program_id(1)

    @pl.when(e == 0)
    def _():
        o_ref[...] = jnp.zeros_like(o_ref)

    half_groups = groups // 2
    half_rows = half_groups * PEER_NKEYS
    pres = []
    for half in range(2):
        rows = pl.ds(half * half_rows, half_rows)
        pres.append(lax.dot_general(u_ref[rows, :], x_ref[...], (((1,), (1,)), ((), ())),
                                    preferred_element_type=jnp.float32))
    for half in range(2):
        acts = []
        for g in range(half_groups):
            pre = pres[half][g * PEER_NKEYS:(g + 1) * PEER_NKEYS]
            i = e * groups + half * half_groups + g
            w = jnp.zeros_like(pre)
            for h in range(PEER_HEADS):
                thr = thr_ref[h, pl.ds(i, 1), :]
                a = a_ref[h, pl.ds(i, 1), :]
                w = w + jnp.where(s1_ref[h] >= thr, b_ref[h], 0.0) * a
            acts.append((jax.nn.gelu(pre) * w).astype(jnp.bfloat16))
        act = jnp.concatenate(acts, axis=0)
        rows = pl.ds(half * half_rows, half_rows)
        o_ref[...] += lax.dot_general(act, v_ref[rows, :], (((0,), (0,)), ((), ())),
                                      preferred_element_type=jnp.float32)


PEER_DENSE_TOKENS = 512
PEER_DENSE_GROUPS = 4


def peer_dense(x, u, v, s1, b, thr, a):
    T, D = x.shape
    bt = min(PEER_DENSE_TOKENS, T)
    groups = PEER_DENSE_GROUPS
    be = groups * PEER_NKEYS
    once = pl.Buffered(1)
    rspec = pl.BlockSpec((PEER_HEADS, PEER_NKEYS, bt), lambda t, e: (0, 0, t), pipeline_mode=once)
    return pl.pallas_call(
        functools.partial(_peer_dense_kernel, groups=groups),
        grid=(T // bt, PEER_N // be),
        in_specs=[pl.BlockSpec((bt, D), lambda t, e: (t, 0), pipeline_mode=once),
                  pl.BlockSpec((be, D), lambda t, e: (e, 0)),
                  pl.BlockSpec((be, D), lambda t, e: (e, 0)),
                  rspec, rspec, rspec, rspec],
        out_specs=pl.BlockSpec((bt, D), lambda t, e: (t, 0)),
        out_shape=jax.ShapeDtypeStruct((T, D), jnp.float32),
        compiler_params=pltpu.CompilerParams(
            dimension_semantics=("parallel", "arbitrary"),
            vmem_limit_bytes=VMEM_LIMIT_BYTES),
        name="peer_dense",
    )(x, u, v, s1, b, thr, a)


def peer(h, w_q, sub_keys, exp_u, exp_v):
    B, L, D = h.shape
    x = h.reshape(B * L, D).astype(jnp.bfloat16)
    qt = mm_nt(w_q.T, x)
    s1, b, thr, a = peer_route(qt, sub_keys)
    return peer_dense(x, exp_u, exp_v, s1, b, thr, a).reshape(B, L, D)


def mixer(kv, rest, p, log_f, log_b, s_f, s_b, nk_c, nv_c, latent):
    B, L, _ = kv.shape
    hy = hyena_mixer(rest[..., R_HY:R_RQ], p['hy_conv_w'], p['hy_conv_b'], p['hy_f'], p['hy_skip'])
    rq = split_heads(rest[..., R_RQ:R_RG])
    rk = split_heads(kv[..., K_RK:K_RV]) * RET_SCALE
    rv = split_heads(kv[..., K_RV:K_NK])
    if latent:
        rq = axial_rope(rq)
        rk = axial_rope(rk)
    ro = retention_bidir(to_bhld(rq), to_bhld(rk), to_bhld(rv), log_f, log_b, s_f, s_b)
    ro = head_rms(jnp.swapaxes(ro, 1, 2)).reshape(B, L, RET_W).astype(kv.dtype)
    ro = ro * jax.nn.silu(rest[..., R_RG:R_NQ])
    nq = rmsnorm(split_heads(rest[..., R_NQ:]), p['q_norm'])
    if latent:
        nk = rmsnorm(split_heads(kv[..., K_NK:K_NV]), p['k_norm'])
        nv = split_heads(kv[..., K_NV:])
        no = na_latent(nq, nk, nv, nk_c, nv_c, p['na_rpb'])
    else:
        no = ctx_attention(nq, nk_c, nv_c)
    return mm(jnp.concatenate([hy, ro, no.reshape(B, L, NA_W)], axis=-1), p['w_out'])


def layer(xl, xc, mod_l, mod_c, p, last):
    sa_l, ca_l, ga_l, sf_l, cf_l, gf_l = jnp.split(mod_l, N_MOD, axis=-1)
    mod_c_parts = jnp.split(mod_c, mod_c.shape[-1] // D_MODEL, axis=-1)
    sa_c, ca_c = mod_c_parts[0], mod_c_parts[1]
    log_f = -jnp.exp(p['ret_decay'][0].astype(jnp.float32))
    log_b = -jnp.exp(p['ret_decay'][1].astype(jnp.float32))
    w_in = p['w_in']
    hc = modulate(rmsnorm(xc, p['norm_mix']), sa_c, ca_c)
    kv_c = mm(hc, w_in[:, :KV_COLS])
    rk_c = to_bhld(split_heads(kv_c[..., K_RK:K_RV]) * RET_SCALE)
    rv_c = to_bhld(split_heads(kv_c[..., K_RV:K_NK]))
    s_f = ret_final_state(rk_c, rv_c, log_f, False)
    s_b = ret_final_state(rk_c, rv_c, log_b, True)
    nk_c = rmsnorm(split_heads(kv_c[..., K_NK:K_NV]), p['k_norm'])
    nv_c = split_heads(kv_c[..., K_NV:])
    hl = modulate(rmsnorm(xl, p['norm_mix']), sa_l, ca_l)
    y_l = mixer(mm(hl, w_in[:, :KV_COLS]), mm(hl, w_in[:, KV_COLS:]), p, log_f, log_b, s_f, s_b, nk_c, nv_c, True)
    xl = xl + ga_l * y_l
    hf_l = modulate(rmsnorm(xl, p['norm_ffn']), sf_l, cf_l)
    xl = xl + gf_l * peer(hf_l, p['peer_wq'], p['peer_keys'], p['peer_u'], p['peer_v'])
    if last:
        return xl, None
    ga_c, sf_c, cf_c, gf_c = mod_c_parts[2], mod_c_parts[3], mod_c_parts[4], mod_c_parts[5]
    zeros = jnp.zeros_like(s_f)
    y_c = mixer(kv_c, mm(hc, w_in[:, KV_COLS:]), p, log_f, log_b, zeros, zeros, nk_c, nv_c, False)
    xc = xc + ga_c * y_c
    hf_c = modulate(rmsnorm(xc, p['norm_ffn']), sf_c, cf_c)
    xc = xc + gf_c * peer(hf_c, p['peer_wq'], p['peer_keys'], p['peer_u'], p['peer_v'])
    return xl, xc


def kernel(x, c, ctx, c_ctx, w_mod, b_mod, norm_mix, w_in, hy_conv_w, hy_conv_b,
           hy_f_w1, hy_f_b1, hy_f_freq, hy_f_w2, hy_f_b2, hy_f_w3, hy_f_b3, hy_f_w4,
           hy_skip, ret_decay, na_rpb, q_norm, k_norm, w_out, norm_ffn,
           peer_wq, peer_keys, peer_u, peer_v):
    xl, xc = x, ctx
    sc = jax.nn.silu(c)
    scc = jax.nn.silu(c_ctx)
    for l in range(DEPTH):
        last = l == DEPTH - 1
        p = {
            'norm_mix': norm_mix[l], 'w_in': w_in[l],
            'hy_conv_w': hy_conv_w[l], 'hy_conv_b': hy_conv_b[l],
            'hy_f': (hy_f_w1[l], hy_f_b1[l], hy_f_freq[l], hy_f_w2[l], hy_f_b2[l],
                     hy_f_w3[l], hy_f_b3[l], hy_f_w4[l]),
            'hy_skip': hy_skip[l], 'ret_decay': ret_decay[l], 'na_rpb': na_rpb[l],
            'q_norm': q_norm[l], 'k_norm': k_norm[l], 'w_out': w_out[l],
            'norm_ffn': norm_ffn[l], 'peer_wq': peer_wq[l], 'peer_keys': peer_keys[l],
            'peer_u': peer_u[l].astype(jnp.bfloat16), 'peer_v': peer_v[l].astype(jnp.bfloat16),
        }
        mod_l = (sc @ w_mod[l] + b_mod[l])[:, None, :]
        n_ctx = 2 if last else N_MOD
        mod_c = (scc @ w_mod[l][:, :n_ctx * D_MODEL] + b_mod[l][:n_ctx * D_MODEL])[None, None, :]
        xl, xc = layer(xl, xc, mod_l, mod_c, p, last)
    return xl
```

```python
import functools
import math

import jax
import jax.numpy as jnp
from jax import lax
from jax.experimental import pallas as pl
from jax.experimental.pallas import tpu as pltpu

D_MODEL = 4096
BATCH = 4
SEQ = 4096
DEPTH = 2

GRID_W = 64
CTX_LEN = 256
HEAD_DIM = 128
N_MIX_HEADS = D_MODEL // HEAD_DIM
HY_GROUPS = N_MIX_HEADS // 4
RET_HEADS = (N_MIX_HEADS - HY_GROUPS) // 2
NA_HEADS = N_MIX_HEADS - HY_GROUPS - RET_HEADS
HY_W = HY_GROUPS * HEAD_DIM
RET_W = RET_HEADS * HEAD_DIM
NA_W = NA_HEADS * HEAD_DIM
K_RK = 0
K_RV = K_RK + RET_W
K_NK = K_RV + RET_W
K_NV = K_NK + NA_W
KV_COLS = K_NV + NA_W
R_HY = 0
R_RQ = R_HY + 3 * HY_W
R_RG = R_RQ + RET_W
R_NQ = R_RG + RET_W
REST_COLS = R_NQ + NA_W
IN_COLS = KV_COLS + REST_COLS
N_MOD = 6
HY_ORDER = 2
HY_SHORT = 3
HY_EMB = 33
HY_BANDS = (HY_EMB - 1) // 2
HY_FILTER_HIDDEN = 64
HY_DECAY_TARGET = 1e-2
HY_SHORT_DECAY_PCT = 0.3
HY_LONG_DECAY_PCT = 1.5
HY_MAX_DECAY = math.log(HY_DECAY_TARGET) / HY_SHORT_DECAY_PCT
HY_MIN_DECAY = math.log(HY_DECAY_TARGET) / HY_LONG_DECAY_PCT
HY_FILTER_SCALE = 0.05
RET_CHUNK = 128
RET_SCALE = HEAD_DIM ** -0.5
ROPE_BASE = 10000.0
NA_KH = 8
NA_KW = 16
PEER_HEADS = 8
PEER_NKEYS = 128
PEER_N = PEER_NKEYS * PEER_NKEYS
PEER_TOPK = 16
PEER_DK = 256
PEER_BLOCK = 64
EPS = 1e-6
NEG_INF = -1e30

VMEM_LIMIT_BYTES = 56 * 1024 * 1024


def _mm_kernel(a_ref, b_ref, o_ref):
    o_ref[...] = jnp.dot(a_ref[...], b_ref[...], preferred_element_type=jnp.float32)


def _pick_block(n, candidates):
    for c in candidates:
        if n % c == 0:
            return c
    return n


def mm(a, b):
    lead = a.shape[:-1]
    K = a.shape[-1]
    a2 = a.reshape(-1, K).astype(jnp.bfloat16)
    b2 = b.astype(jnp.bfloat16)
    M, N = a2.shape[0], b2.shape[1]
    bm = _pick_block(M, (1024, 512, 256, 128, 64, 32, 16, 8))
    bn = _pick_block(N, (1024, 768, 512, 256, 128))
    out = pl.pallas_call(
        _mm_kernel,
        grid=(M // bm, N // bn),
        in_specs=[pl.BlockSpec((bm, K), lambda i, j: (i, 0)),
                  pl.BlockSpec((K, bn), lambda i, j: (0, j))],
        out_specs=pl.BlockSpec((bm, bn), lambda i, j: (i, j)),
        out_shape=jax.ShapeDtypeStruct((M, N), jnp.float32),
        compiler_params=pltpu.CompilerParams(
            dimension_semantics=("parallel", "parallel"),
            vmem_limit_bytes=VMEM_LIMIT_BYTES),
        name="mm",
    )(a2, b2)
    return out.reshape(lead + (N,))


def rmsnorm(x, g):
    xf = x.astype(jnp.float32)
    y = xf * lax.rsqrt(jnp.mean(xf * xf, axis=-1, keepdims=True) + EPS)
    return (y * g.astype(jnp.float32)).astype(x.dtype)


def head_rms(x):
    xf = x.astype(jnp.float32)
    return xf * lax.rsqrt(jnp.mean(xf * xf, axis=-1, keepdims=True) + EPS)


def modulate(h, shift, scale):
    return h * (1.0 + scale) + shift


def split_heads(x):
    return x.reshape(x.shape[:-1] + (x.shape[-1] // HEAD_DIM, HEAD_DIM))


def to_bhld(x):
    return jnp.swapaxes(x, 1, 2).astype(jnp.float32)


def axial_rope(x):
    L = x.shape[1]
    t = jnp.arange(L)
    row = (t // GRID_W).astype(jnp.float32)
    col = (t % GRID_W).astype(jnp.float32)
    quarter = x.shape[-1] // 4
    inv = ROPE_BASE ** (-jnp.arange(quarter, dtype=jnp.float32) / quarter)

    def rot(xp, pos):
        ang = pos[:, None] * inv[None, :]
        cos = jnp.cos(ang)[None, :, None, :]
        sin = jnp.sin(ang)[None, :, None, :]
        a, b = xp[..., :quarter], xp[..., quarter:]
        return jnp.concatenate([a * cos - b * sin, a * sin + b * cos], axis=-1)

    half = 2 * quarter
    return jnp.concatenate([rot(x[..., :half], row), rot(x[..., half:], col)], axis=-1).astype(x.dtype)


def short_conv(u, w, b):
    C = u.shape[-1]
    y = lax.conv_general_dilated(u, w[:, None, :].astype(u.dtype), window_strides=(1,),
                                 padding=[(HY_SHORT // 2, HY_SHORT // 2)],
                                 dimension_numbers=('NWC', 'WIO', 'NWC'),
                                 feature_group_count=C)
    return y + b.astype(u.dtype)


def hyena_filters(L, w1, b1, freq, w2, b2, w3, b3, w4):
    f32 = lambda a: a.astype(jnp.float32)
    t = jnp.linspace(0.0, 1.0, L, dtype=jnp.float32)[:, None]
    w = 2.0 * math.pi * jnp.arange(L, dtype=jnp.float32)[:, None] / L
    f = jnp.linspace(1e-4, HY_BANDS - 1, HY_BANDS, dtype=jnp.float32)[None, :]
    z = jnp.concatenate([t, jnp.cos(f * w), -jnp.sin(f * w)], axis=-1)
    h = jnp.sin(f32(freq) * (z @ f32(w1) + f32(b1)))
    h = jnp.sin(f32(freq) * (h @ f32(w2) + f32(b2)))
    h = jnp.sin(f32(freq) * (h @ f32(w3) + f32(b3)))
    k = (h @ f32(w4)).reshape(L, HY_ORDER, 2, HY_W)
    deltas = jnp.abs(jnp.linspace(HY_MIN_DECAY, HY_MAX_DECAY, HY_W, dtype=jnp.float32))
    return k * jnp.exp(-t[:, :, None, None] * deltas)


def long_conv_bidir(u, kf, kb, skip):
    L, C = u.shape[1], u.shape[2]
    filt = jnp.concatenate([kf, jnp.zeros((1, C), jnp.float32), kb[:0:-1]], axis=0)
    uf = u.astype(jnp.float32)
    U = jnp.fft.rfft(uf, n=2 * L, axis=1)
    K = jnp.fft.rfft(filt, n=2 * L, axis=0)
    y = jnp.fft.irfft(U * K[None], n=2 * L, axis=1)[:, :L]
    return (y + uf * skip.astype(jnp.float32)).astype(u.dtype)


def hyena_mixer(u, conv_w, conv_b, filt, skip):
    L = u.shape[1]
    u = short_conv(u, conv_w, conv_b)
    v, x1, x2 = jnp.split(u, 3, axis=-1)
    k = hyena_filters(L, *filt)
    z = x1 * long_conv_bidir(v, k[:, 0, 0], k[:, 0, 1], skip[0])
    z = x2 * long_conv_bidir(z, k[:, 1, 0], k[:, 1, 1], skip[1])
    return z


def ret_chunkwise(q, k, v, log_g, s0):
    B, H, L, d = q.shape
    C = RET_CHUNK
    N = L // C
    pos = jnp.arange(C, dtype=jnp.float32)
    diff = pos[:, None] - pos[None, :]
    inner_decay = jnp.where(diff >= 0, jnp.exp(log_g[:, None, None] * jnp.maximum(diff, 0.0)), 0.0)
    q_decay = jnp.exp(log_g[:, None] * (pos + 1.0))[:, :, None]
    k_decay = jnp.exp(log_g[:, None] * (C - 1.0 - pos))[:, :, None]
    chunk_decay = jnp.exp(log_g * C)[:, None, None]

    def chunks(a):
        return jnp.moveaxis(a.reshape(B, H, N, C, d), 2, 0)

    def step(S, qkv):
        qn, kn, vn = qkv
        a = jnp.einsum('bhid,bhjd->bhij', qn, kn) * inner_decay
        o = jnp.einsum('bhij,bhje->bhie', a, vn) + jnp.einsum('bhid,bhde->bhie', qn * q_decay, S)
        S = S * chunk_decay + jnp.einsum('bhjd,bhje->bhde', kn * k_decay, vn)
        return S, o

    _, o = lax.scan(step, s0, (chunks(q), chunks(k), chunks(v)))
    return jnp.moveaxis(o, 0, 2).reshape(B, H, L, d)


def ret_final_state(k, v, log_g, reverse):
    L = k.shape[2]
    pos = jnp.arange(L, dtype=jnp.float32)
    expo = pos if reverse else (L - 1.0 - pos)
    w = jnp.exp(log_g[:, None] * expo)
    return jnp.einsum('bhld,bhle->bhde', k * w[None, :, :, None], v)


def retention_bidir(q, k, v, log_f, log_b, s_f, s_b):
    fwd = ret_chunkwise(q, k, v, log_f, s_f)
    bwd = jnp.flip(ret_chunkwise(jnp.flip(q, 2), jnp.flip(k, 2), jnp.flip(v, 2), log_b, s_b), 2)
    return fwd + bwd


def na_latent(q, k, v, kc, vc, rpb):
    B, L, H, d = q.shape
    rows = L // GRID_W
    kh = min(NA_KH, rows)
    kw = NA_KW
    scale = d ** -0.5
    col = jnp.arange(GRID_W)
    cstart = jnp.clip(col - kw // 2, 0, GRID_W - kw)
    cmask = (col[None, :] >= cstart[:, None]) & (col[None, :] < cstart[:, None] + kw)
    cidx = jnp.clip(col[None, :] - col[:, None] + kw - 1, 0, 2 * kw - 2)
    rpb_c = rpb[:, :, cidx]
    kg = k.reshape(B, rows, GRID_W, H, d)
    vg = v.reshape(B, rows, GRID_W, H, d)
    qg = q.reshape(B, rows, GRID_W, H, d)

    def row_block(args):
        r, q_r = args
        start = jnp.clip(r - kh // 2, 0, rows - kh)
        k_b = lax.dynamic_slice_in_dim(kg, start, kh, axis=1)
        v_b = lax.dynamic_slice_in_dim(vg, start, kh, axis=1)
        ridx = start + jnp.arange(kh) - r + NA_KH - 1
        bias = jnp.transpose(rpb_c[:, ridx], (0, 2, 1, 3)).astype(jnp.float32)
        s_loc = jnp.einsum('bqhd,bikhd->bhqik', q_r, k_b).astype(jnp.float32) * scale + bias[None]
        s_loc = jnp.where(cmask[None, None, :, None, :], s_loc, NEG_INF).reshape(B, H, GRID_W, kh * GRID_W)
        s_ctx = jnp.einsum('bqhd,bchd->bhqc', q_r, kc).astype(jnp.float32) * scale
        p = jax.nn.softmax(jnp.concatenate([s_loc, s_ctx], axis=-1), axis=-1)
        p_loc = p[..., :kh * GRID_W].reshape(B, H, GRID_W, kh, GRID_W).astype(v.dtype)
        p_ctx = p[..., kh * GRID_W:].astype(v.dtype)
        return (jnp.einsum('bhqik,bikhd->bqhd', p_loc, v_b)
                + jnp.einsum('bhqc,bchd->bqhd', p_ctx, vc))

    out = lax.map(row_block, (jnp.arange(rows), jnp.moveaxis(qg, 1, 0)))
    return jnp.moveaxis(out, 0, 1).reshape(B, L, H, d)


def ctx_attention(q, k, v):
    s = jnp.einsum('bqhd,bkhd->bhqk', q, k).astype(jnp.float32) * q.shape[-1] ** -0.5
    p = jax.nn.softmax(s, axis=-1).astype(v.dtype)
    return jnp.einsum('bhqk,bkhd->bqhd', p, v)


def _mm_nt_kernel(w_ref, x_ref, o_ref):
    o_ref[...] = lax.dot_general(w_ref[...], x_ref[...], (((1,), (1,)), ((), ())),
                                 preferred_element_type=jnp.float32)


def mm_nt(w, x):
    N, K = w.shape
    M = x.shape[0]
    bn = _pick_block(N, (1024, 512, 256, 128))
    bm = _pick_block(M, (1024, 512, 256, 128))
    return pl.pallas_call(
        _mm_nt_kernel,
        grid=(M // bm, N // bn),
        in_specs=[pl.BlockSpec((bn, K), lambda i, j: (j, 0)),
                  pl.BlockSpec((bm, K), lambda i, j: (i, 0))],
        out_specs=pl.BlockSpec((bn, bm), lambda i, j: (j, i)),
        out_shape=jax.ShapeDtypeStruct((N, M), jnp.float32),
        compiler_params=pltpu.CompilerParams(
            dimension_semantics=("parallel", "parallel"),
            vmem_limit_bytes=VMEM_LIMIT_BYTES),
        name="mm_nt",
    )(w.astype(jnp.bfloat16), x.astype(jnp.bfloat16))


def _top_values(s, k):
    vals = []
    work = s
    for _ in range(k):
        m = jnp.max(work, axis=0, keepdims=True)
        vals.append(m)
        work = jnp.where(work == m, -jnp.inf, work)
    return jnp.concatenate(vals, axis=0)


def _peer_route_kernel(qt_ref, keys_ref, s1_ref, b_ref, thr_ref, a_ref):
    half = PEER_DK // 2
    K = PEER_TOPK
    for h in range(PEER_HEADS):
        s = []
        for p in range(2):
            q = qt_ref[pl.ds((2 * h + p) * half, half), :]
            s.append(jnp.dot(keys_ref[2 * h + p], q, preferred_element_type=jnp.float32,
                             precision=lax.Precision.HIGHEST))
        s0, s1 = s
        v0 = _top_values(s0, K)
        v1 = _top_values(s1, K)
        cand = jnp.concatenate([v0[a:a + 1] + v1 for a in range(K)], axis=0)
        tau = _top_values(cand, K)[K - 1:K]
        m = v0[0:1] + v1[0:1]
        z = jnp.sum(jnp.where(cand >= tau, jnp.exp(cand - m), 0.0), axis=0, keepdims=True)
        thr = jnp.full_like(s0, jnp.inf)
        for b in range(K):
            vb = v1[b:b + 1]
            thr = jnp.minimum(thr, jnp.where(s0 + vb >= tau, vb, jnp.inf))
        s1_ref[h] = s1
        b_ref[h] = jnp.exp(s1 - v1[0:1])
        thr_ref[h] = thr
        a_ref[h] = jnp.exp(s0 - v0[0:1]) / z


PEER_ROUTE_LANES = 128


def peer_route(qt, sub_keys):
    T = qt.shape[1]
    keys = sub_keys.reshape(PEER_HEADS * 2, PEER_NKEYS, PEER_DK // 2).astype(jnp.float32)
    bt = PEER_ROUTE_LANES
    out = jax.ShapeDtypeStruct((PEER_HEADS, PEER_NKEYS, T), jnp.float32)
    ospec = pl.BlockSpec((PEER_HEADS, PEER_NKEYS, bt), lambda i: (0, 0, i))
    return pl.pallas_call(
        _peer_route_kernel,
        grid=(T // bt,),
        in_specs=[pl.BlockSpec((PEER_HEADS * PEER_DK, bt), lambda i: (0, i)),
                  pl.BlockSpec(keys.shape, lambda i: (0, 0, 0))],
        out_specs=[ospec] * 4,
        out_shape=[out] * 4,
        compiler_params=pltpu.CompilerParams(
            dimension_semantics=("parallel",), vmem_limit_bytes=VMEM_LIMIT_BYTES),
        name="peer_route",
    )(qt, keys)


def _peer_dense_kernel(x_ref, u_ref, v_ref, s1_ref, b_ref, thr_ref, a_ref, o_ref, *, groups):
    e = pl.program_id(1)

    @pl.when(e == 0)
    def _():
        o_ref[...] = jnp.zeros_like(o_ref)

    half_groups = groups // 2
    half_rows = half_groups * PEER_NKEYS
    pres = []
    for half in range(2):
        rows = pl.ds(half * half_rows, half_rows)
        pres.append(lax.dot_general(u_ref[rows, :], x_ref[...], (((1,), (1,)), ((), ())),
                                    preferred_element_type=jnp.float32))
    for half in range(2):
        acts = []
        for g in range(half_groups):
            pre = pres[half][g * PEER_NKEYS:(g + 1) * PEER_NKEYS]
            i = e * groups + half * half_groups + g
            w = jnp.zeros_like(pre)
            for h in range(PEER_HEADS):
                thr = thr_ref[h, pl.ds(i, 1), :]
                a = a_ref[h, pl.ds(i, 1), :]
                w = w + jnp.where(s1_ref[h] >= thr, b_ref[h], 0.0) * a
            acts.append((jax.nn.gelu(pre) * w).astype(jnp.bfloat16))
        act = jnp.concatenate(acts, axis=0)
        rows = pl.ds(half * half_rows, half_rows)
        o_ref[...] += lax.dot_general(act, v_ref[rows, :], (((0,), (0,)), ((), ())),
                                      preferred_element_type=jnp.float32)


PEER_DENSE_TOKENS = 512
PEER_DENSE_GROUPS = 4


def peer_dense(x, u, v, s1, b, thr, a):
    T, D = x.shape
    bt = min(PEER_DENSE_TOKENS, T)
    groups = PEER_DENSE_GROUPS
    be = groups * PEER_NKEYS
    once = pl.Buffered(1)
    rspec = pl.BlockSpec((PEER_HEADS, PEER_NKEYS, bt), lambda t, e: (0, 0, t), pipeline_mode=once)
    return pl.pallas_call(
        functools.partial(_peer_dense_kernel, groups=groups),
        grid=(T // bt, PEER_N // be),
        in_specs=[pl.BlockSpec((bt, D), lambda t, e: (t, 0), pipeline_mode=once),
                  pl.BlockSpec((be, D), lambda t, e: (e, 0)),
                  pl.BlockSpec((be, D), lambda t, e: (e, 0)),
                  rspec, rspec, rspec, rspec],
        out_specs=pl.BlockSpec((bt, D), lambda t, e: (t, 0)),
        out_shape=jax.ShapeDtypeStruct((T, D), jnp.float32),
        compiler_params=pltpu.CompilerParams(
            dimension_semantics=("parallel", "arbitrary"),
            vmem_limit_bytes=VMEM_LIMIT_BYTES),
        name="peer_dense",
    )(x, u, v, s1, b, thr, a)


def peer(h, w_q, sub_keys, exp_u, exp_v):
    B, L, D = h.shape
    x = h.reshape(B * L, D).astype(jnp.bfloat16)
    qt = mm_nt(w_q.T, x)
    s1, b, thr, a = peer_route(qt, sub_keys)
    return peer_dense(x, exp_u, exp_v, s1, b, thr, a).reshape(B, L, D)


def _col_spec(L, col0, width=HEAD_DIM):
    assert col0 % width == 0
    return pl.BlockSpec((1, L, width), lambda b, j: (b, 0, col0 // width + j))


def _bf16_dot(a, b, dims):
    return lax.dot_general(a.astype(jnp.bfloat16), b.astype(jnp.bfloat16), (dims, ((), ())),
                           preferred_element_type=jnp.float32)


_NN = ((1,), (0,))
_NT = ((1,), (1,))
_TN = ((0,), (0,))


def rope_tables(L):
    quarter = HEAD_DIM // 4
    t = jnp.arange(L)
    lane = jnp.arange(HEAD_DIM)
    pos = jnp.where(lane[None, :] < 2 * quarter, (t // GRID_W)[:, None], (t % GRID_W)[:, None]).astype(jnp.float32)
    inv = ROPE_BASE ** (-(lane % quarter).astype(jnp.float32) / quarter)
    ang = pos * inv[None, :]
    sign = jnp.where((lane % (2 * quarter)) < quarter, -1.0, 1.0)
    return jnp.cos(ang), jnp.sin(ang) * sign[None, :]


def _rope(x, cos, sin_signed):
    quarter = HEAD_DIM // 4
    lane = lax.broadcasted_iota(jnp.int32, x.shape, 1)
    partner = jnp.where((lane % (2 * quarter)) < quarter,
                        pltpu.roll(x, HEAD_DIM - quarter, axis=1), pltpu.roll(x, quarter, axis=1))
    return x * cos + partner * sin_signed


def _ret_kernel(*refs, n_chunks, rope):
    if rope:
        (q_ref, k_ref, v_ref, g_ref, sf0_ref, sb0_ref, dec_ref, cos_ref, sin_ref,
         o_ref, sf_ref, sb_ref, kvf, kvb, kr) = refs
    else:
        (q_ref, k_ref, v_ref, g_ref, sf0_ref, sb0_ref, dec_ref,
         o_ref, sf_ref, sb_ref, kvf, kvb, kr) = refs
    C = RET_CHUNK
    lf = dec_ref[0, 0:1, :]
    lb = dec_ref[0, 1:2, :]
    pos = lax.broadcasted_iota(jnp.int32, (C, HEAD_DIM), 0).astype(jnp.float32)
    kdf = jnp.exp(lf * (C - 1.0 - pos))
    kdb = jnp.exp(lb * pos)
    qdf = jnp.exp(lf * (pos + 1.0))
    qdb = jnp.exp(lb * (C - pos))
    cf = jnp.exp(lf * C)
    cb = jnp.exp(lb * C)
    diff = (lax.broadcasted_iota(jnp.int32, (C, C), 0) - lax.broadcasted_iota(jnp.int32, (C, C), 1)).astype(jnp.float32)
    dmat = (jnp.where(diff >= 0, jnp.exp(lf * jnp.maximum(diff, 0.0)), 0.0)
            + jnp.where(diff <= 0, jnp.exp(lb * jnp.maximum(-diff, 0.0)), 0.0))

    def rows(m):
        return pl.ds(pl.multiple_of(m * C, C), C)

    def summaries(m, carry):
        k = k_ref[0, rows(m), :] * RET_SCALE
        if rope:
            k = _rope(k, cos_ref[rows(m), :], sin_ref[rows(m), :])
        kr[rows(m), :] = k
        v = v_ref[0, rows(m), :]
        kvf[m] = _bf16_dot(k * kdf, v, _TN)
        kvb[m] = _bf16_dot(k * kdb, v, _TN)
        return carry

    lax.fori_loop(0, n_chunks, summaries, 0)

    def scan_f(m, s):
        nxt = s * cf + kvf[m]
        kvf[m] = s
        return nxt

    sf_ref[0, 0] = lax.fori_loop(0, n_chunks, scan_f, sf0_ref[0, 0])

    def scan_b(i, s):
        m = n_chunks - 1 - i
        nxt = s * cb + kvb[m]
        kvb[m] = s
        return nxt

    sb_ref[0, 0] = lax.fori_loop(0, n_chunks, scan_b, sb0_ref[0, 0])

    def outputs(m, carry):
        q = q_ref[0, rows(m), :]
        if rope:
            q = _rope(q, cos_ref[rows(m), :], sin_ref[rows(m), :])
        k = kr[rows(m), :]
        v = v_ref[0, rows(m), :]
        a = _bf16_dot(q, k, _NT) * dmat
        o = (_bf16_dot(a, v, _NN) + _bf16_dot(q * qdf, kvf[m], _NN) + _bf16_dot(q * qdb, kvb[m], _NN))
        o = o * lax.rsqrt(jnp.mean(o * o, axis=-1, keepdims=True) + EPS)
        o_ref[0, rows(m), :] = (o * jax.nn.silu(g_ref[0, rows(m), :])).astype(o_ref.dtype)
        return carry

    lax.fori_loop(0, n_chunks, outputs, 0)


def retention(proj, log_f, log_b, s_f, s_b, rope):
    B, L, _ = proj.shape
    n_chunks = L // RET_CHUNK
    dec = jnp.zeros((RET_HEADS, 8, HEAD_DIM), jnp.float32)
    dec = dec.at[:, 0, :].set(log_f[:, None]).at[:, 1, :].set(log_b[:, None])
    state_spec = pl.BlockSpec((1, 1, HEAD_DIM, HEAD_DIM), lambda b, h: (b, h, 0, 0))
    in_specs = [_col_spec(L, KV_COLS + R_RQ), _col_spec(L, K_RK), _col_spec(L, K_RV), _col_spec(L, KV_COLS + R_RG),
                state_spec, state_spec, pl.BlockSpec((1, 8, HEAD_DIM), lambda b, h: (h, 0, 0))]
    args = [proj, proj, proj, proj, s_f, s_b, dec]
    if rope:
        tab_spec = pl.BlockSpec((L, HEAD_DIM), lambda b, h: (0, 0))
        in_specs += [tab_spec, tab_spec]
        args += list(rope_tables(L))
    state_shape = jax.ShapeDtypeStruct((B, RET_HEADS, HEAD_DIM, HEAD_DIM), jnp.float32)
    return pl.pallas_call(
        functools.partial(_ret_kernel, n_chunks=n_chunks, rope=rope),
        grid=(B, RET_HEADS),
        in_specs=in_specs,
        out_specs=[_col_spec(L, 0), state_spec, state_spec],
        out_shape=[jax.ShapeDtypeStruct((B, L, RET_W), jnp.bfloat16), state_shape, state_shape],
        scratch_shapes=[pltpu.VMEM((n_chunks, HEAD_DIM, HEAD_DIM), jnp.float32),
                        pltpu.VMEM((n_chunks, HEAD_DIM, HEAD_DIM), jnp.float32),
                        pltpu.VMEM((L, HEAD_DIM), jnp.float32)],
        compiler_params=pltpu.CompilerParams(
            dimension_semantics=("parallel", "parallel"), vmem_limit_bytes=VMEM_LIMIT_BYTES),
        name="retention",
    )(*args)


def na_bias_windows(rpb):
    col = jnp.arange(GRID_W)
    cstart = jnp.clip(col - NA_KW // 2, 0, GRID_W - NA_KW)
    cmask = (col[None, :] >= cstart[:, None]) & (col[None, :] < cstart[:, None] + NA_KW)
    cidx = jnp.clip(col[None, :] - col[:, None] + NA_KW - 1, 0, 2 * NA_KW - 2)
    rpb_c = jnp.where(cmask[None, None], rpb[:, :, cidx].astype(jnp.float32), NEG_INF)
    wins = [jnp.concatenate([rpb_c[:, w + i] for i in range(NA_KH)], axis=-1) for w in range(NA_KH)]
    return jnp.stack(wins, axis=1)


def _na_kernel(*refs, n_rows, local):
    if local:
        (q_ref, k_ref, v_ref, kc_ref, vc_ref, bw_ref, qn_ref, kn_ref, o_ref, k_s, v_s, kc_s, vc_s) = refs
    else:
        (q_ref, kc_ref, vc_ref, qn_ref, kn_ref, o_ref, kc_s, vc_s) = refs
    scale = HEAD_DIM ** -0.5

    def norm(x, g_ref):
        return x * lax.rsqrt(jnp.mean(x * x, axis=-1, keepdims=True) + EPS) * g_ref[...]

    kc_s[...] = norm(kc_ref[0], kn_ref).astype(jnp.bfloat16)
    vc_s[...] = vc_ref[0].astype(jnp.bfloat16)
    if local:
        k_s[...] = norm(k_ref[0], kn_ref).astype(jnp.bfloat16)
        v_s[...] = v_ref[0].astype(jnp.bfloat16)

    def row(r, carry):
        qrows = pl.ds(pl.multiple_of(r * GRID_W, GRID_W), GRID_W)
        q = norm(q_ref[0, qrows, :], qn_ref).astype(jnp.bfloat16)
        s_ctx = _bf16_dot(q, kc_s[...], _NT) * scale
        m = jnp.max(s_ctx, axis=-1, keepdims=True)
        if local:
            start = jnp.clip(r - NA_KH // 2, 0, n_rows - NA_KH)
            krows = pl.ds(pl.multiple_of(start * GRID_W, GRID_W), NA_KH * GRID_W)
            s_loc = _bf16_dot(q, k_s[krows, :], _NT) * scale + bw_ref[0, start - r + NA_KH - 1]
            m = jnp.maximum(m, jnp.max(s_loc, axis=-1, keepdims=True))
            p_loc = jnp.exp(s_loc - m)
        p_ctx = jnp.exp(s_ctx - m)
        den = jnp.sum(p_ctx, axis=-1, keepdims=True)
        o = _bf16_dot(p_ctx, vc_s[...], _NN)
        if local:
            den = den + jnp.sum(p_loc, axis=-1, keepdims=True)
            o = o + _bf16_dot(p_loc, v_s[krows, :], _NN)
        o_ref[0, qrows, :] = (o / den).astype(o_ref.dtype)
        return carry

    lax.fori_loop(0, n_rows, row, 0)


def neighbourhood_attention(proj, proj_c, rpb, q_norm, k_norm, local):
    B, L, _ = proj.shape
    Lc = proj_c.shape[1]
    n_rows = L // GRID_W
    norm_spec = pl.BlockSpec((1, HEAD_DIM), lambda b, h: (0, 0))
    qn = q_norm.reshape(1, HEAD_DIM).astype(jnp.float32)
    kn = k_norm.reshape(1, HEAD_DIM).astype(jnp.float32)
    ctx_specs = [_col_spec(Lc, K_NK), _col_spec(Lc, K_NV)]
    ctx_scratch = [pltpu.VMEM((Lc, HEAD_DIM), jnp.bfloat16), pltpu.VMEM((Lc, HEAD_DIM), jnp.bfloat16)]
    if local:
        bw = na_bias_windows(rpb)
        in_specs = ([_col_spec(L, KV_COLS + R_NQ), _col_spec(L, K_NK), _col_spec(L, K_NV)] + ctx_specs
                    + [pl.BlockSpec((1, NA_KH, GRID_W, NA_KH * GRID_W), lambda b, h: (h, 0, 0, 0)), norm_spec, norm_spec])
        args = [proj, proj, proj, proj_c, proj_c, bw, qn, kn]
        scratch = [pltpu.VMEM((L, HEAD_DIM), jnp.bfloat16), pltpu.VMEM((L, HEAD_DIM), jnp.bfloat16)] + ctx_scratch
    else:
        in_specs = [_col_spec(L, KV_COLS + R_NQ)] + ctx_specs + [norm_spec, norm_spec]
        args = [proj, proj_c, proj_c, qn, kn]
        scratch = ctx_scratch
    return pl.pallas_call(
        functools.partial(_na_kernel, n_rows=n_rows, local=local),
        grid=(B, NA_HEADS),
        in_specs=in_specs,
        out_specs=_col_spec(L, 0),
        out_shape=jax.ShapeDtypeStruct((B, L, NA_W), jnp.bfloat16),
        scratch_shapes=scratch,
        compiler_params=pltpu.CompilerParams(
            dimension_semantics=("parallel", "parallel"), vmem_limit_bytes=VMEM_LIMIT_BYTES),
        name="neighbourhood_attention",
    )(*args)


def dft_tables(L):
    n = 2 * L
    f = jnp.arange(L, dtype=jnp.int32)[:, None]
    t = jnp.arange(L, dtype=jnp.int32)[None, :]
    ang = ((f * t) % n).astype(jnp.float32) * (2.0 * math.pi / n)
    cm = jnp.cos(ang)
    nyquist = jnp.where(t % 2 == 0, 1.0, -1.0)
    sm = jnp.where(f == 0, nyquist, jnp.sin(ang))
    w = jnp.where(f == 0, 1.0, 2.0) / n
    bf = jnp.bfloat16
    return cm.astype(bf), sm.astype(bf), (cm * w).T.astype(bf), (sm * w).T.astype(bf)


def hy_filter_spectrum(k, cm, sm):
    L = k.shape[0]
    kf = k[:, :, 0]
    kb = k[:, :, 1].at[0].set(0.0)
    ksum = (kf + kb).reshape(L, -1)
    kdiff = (kf - kb).reshape(L, -1)
    kc = mm(cm, ksum)
    ks = mm(sm, kdiff)
    sign = jnp.where(jnp.arange(L) % 2 == 0, 1.0, -1.0)[:, None]
    ks = ks.at[0].set(jnp.sum(sign * ksum, axis=0))
    split = lambda a: jnp.moveaxis(a.reshape(L, HY_ORDER, -1), 1, 0)
    return split(kc), split(ks)


def _hy_short_kernel(u_ref, w_ref, b_ref, o_ref):
    u = u_ref[0]
    L = u.shape[0]
    t = lax.broadcasted_iota(jnp.int32, u.shape, 0)
    prev = jnp.where(t == 0, 0.0, pltpu.roll(u, 1, axis=0))
    nxt = jnp.where(t == L - 1, 0.0, pltpu.roll(u, L - 1, axis=0))
    o_ref[0] = prev * w_ref[0:1, :] + u * w_ref[1:2, :] + nxt * w_ref[2:3, :] + b_ref[...]


HY_SHORT_COLS = 256


def hy_short(proj, conv_w, conv_b):
    B, L, _ = proj.shape
    cols = HY_SHORT_COLS
    n = 3 * HY_W
    return pl.pallas_call(
        _hy_short_kernel,
        grid=(B, n // cols),
        in_specs=[_col_spec(L, KV_COLS + R_HY, cols),
                  pl.BlockSpec((HY_SHORT, cols), lambda b, j: (0, j)),
                  pl.BlockSpec((1, cols), lambda b, j: (0, j))],
        out_specs=_col_spec(L, 0, cols),
        out_shape=jax.ShapeDtypeStruct((B, L, n), jnp.float32),
        compiler_params=pltpu.CompilerParams(
            dimension_semantics=("parallel", "parallel"), vmem_limit_bytes=VMEM_LIMIT_BYTES),
        name="hy_short",
    )(proj, conv_w.astype(jnp.float32), conv_b.reshape(1, n).astype(jnp.float32))


def _hy_fwd_kernel(cm_ref, sm_ref, u_ref, kc_ref, ks_ref, yc_ref, ys_ref):
    u = u_ref[0]
    uc = jnp.dot(cm_ref[...], u, preferred_element_type=jnp.float32)
    us = jnp.dot(sm_ref[...], u, preferred_element_type=jnp.float32)
    kc = kc_ref[...]
    ks = ks_ref[...]
    fb = uc.shape[0]
    f = pl.program_id(0) * fb + lax.broadcasted_iota(jnp.int32, (fb, 1), 0)
    packed = f == 0
    yc_ref[0] = jnp.where(packed, uc * kc, uc * kc - us * ks).astype(yc_ref.dtype)
    ys_ref[0] = jnp.where(packed, us * ks, uc * ks + us * kc).astype(ys_ref.dtype)


HY_DFT_ROWS = 512


def hy_fwd(u, cm, sm, kc, ks):
    B, L, C = u.shape
    fb = min(HY_DFT_ROWS, L)
    mat_spec = pl.BlockSpec((fb, L), lambda f, b: (f, 0))
    k_spec = pl.BlockSpec((fb, C), lambda f, b: (f, 0))
    y_spec = pl.BlockSpec((1, fb, C), lambda f, b: (b, f, 0))
    y_shape = jax.ShapeDtypeStruct((B, L, C), jnp.bfloat16)
    return pl.pallas_call(
        _hy_fwd_kernel,
        grid=(L // fb, B),
        in_specs=[mat_spec, mat_spec, pl.BlockSpec((1, L, C), lambda f, b: (b, 0, 0)), k_spec, k_spec],
        out_specs=[y_spec, y_spec],
        out_shape=[y_shape, y_shape],
        compiler_params=pltpu.CompilerParams(
            dimension_semantics=("parallel", "parallel"), vmem_limit_bytes=VMEM_LIMIT_BYTES),
        name="hy_fwd",
    )(cm, sm, u, kc, ks)


def _hy_inv_kernel(gc_ref, gs_ref, yc_ref, ys_ref, u_ref, gate_ref, skip_ref, o_ref):
    conv = (jnp.dot(gc_ref[...], yc_ref[0], preferred_element_type=jnp.float32)
            + jnp.dot(gs_ref[...], ys_ref[0], preferred_element_type=jnp.float32))
    o_ref[0] = (gate_ref[0] * (conv + u_ref[0] * skip_ref[...])).astype(o_ref.dtype)


HY_INV_COLS = 512


def hy_inv(yc, ys, gc, gs, u, u_col0, gate, gate_col0, skip, out_dtype):
    B, L, C = yc.shape
    tb = min(HY_DFT_ROWS, L)
    cols = HY_INV_COLS
    once = pl.Buffered(1)
    mat_spec = pl.BlockSpec((tb, L), lambda b, j, t: (t, 0))
    y_spec = pl.BlockSpec((1, L, cols), lambda b, j, t: (b, 0, j), pipeline_mode=once)

    def win(col0):
        return pl.BlockSpec((1, tb, cols), lambda b, j, t: (b, t, col0 // cols + j))

    return pl.pallas_call(
        _hy_inv_kernel,
        grid=(B, C // cols, L // tb),
        in_specs=[mat_spec, mat_spec, y_spec, y_spec, win(u_col0), win(gate_col0),
                  pl.BlockSpec((1, cols), lambda b, j, t: (0, j))],
        out_specs=win(0),
        out_shape=jax.ShapeDtypeStruct((B, L, C), out_dtype),
        compiler_params=pltpu.CompilerParams(
            dimension_semantics=("parallel", "parallel", "arbitrary"), vmem_limit_bytes=VMEM_LIMIT_BYTES),
        name="hy_inv",
    )(gc, gs, yc, ys, u, gate, skip.reshape(1, C).astype(jnp.float32))


def hyena(proj, conv_w, conv_b, filt, skip, tables):
    B, L, _ = proj.shape
    cm, sm, gc, gs = tables
    hyu = hy_short(proj, conv_w, conv_b)
    kc, ks = hy_filter_spectrum(hyena_filters(L, *filt), cm, sm)
    v = hyu[..., :HY_W].astype(jnp.bfloat16)
    yc, ys = hy_fwd(v, cm, sm, kc[0], ks[0])
    z1 = hy_inv(yc, ys, gc, gs, hyu, 0, hyu, HY_W, skip[0], jnp.float32)
    yc, ys = hy_fwd(z1.astype(jnp.bfloat16), cm, sm, kc[1], ks[1])
    return hy_inv(yc, ys, gc, gs, z1, 0, hyu, 2 * HY_W, skip[1], jnp.bfloat16)


def mixer(proj, proj_c, p, log_f, log_b, s_f, s_b, tables, latent):
    hy = hyena(proj, p['hy_conv_w'], p['hy_conv_b'], p['hy_f'], p['hy_skip'], tables)
    ro, sf, sb = retention(proj, log_f, log_b, s_f, s_b, rope=latent)
    no = neighbourhood_attention(proj, proj_c, p['na_rpb'], p['q_norm'], p['k_norm'], local=latent)
    return mm(jnp.concatenate([hy, ro, no], axis=-1), p['w_out']), sf, sb


def layer(xl, xc, mod_l, mod_c, p, last, tables_l, tables_c):
    sa_l, ca_l, ga_l, sf_l, cf_l, gf_l = jnp.split(mod_l, N_MOD, axis=-1)
    mod_c_parts = jnp.split(mod_c, mod_c.shape[-1] // D_MODEL, axis=-1)
    sa_c, ca_c = mod_c_parts[0], mod_c_parts[1]
    log_f = -jnp.exp(p['ret_decay'][0].astype(jnp.float32))
    log_b = -jnp.exp(p['ret_decay'][1].astype(jnp.float32))
    w_in = p['w_in'].astype(jnp.bfloat16)
    hc = modulate(rmsnorm(xc, p['norm_mix']), sa_c, ca_c)
    proj_c = mm(hc, w_in)
    zeros = jnp.zeros((xc.shape[0], RET_HEADS, HEAD_DIM, HEAD_DIM), jnp.float32)
    if last:
        _, s_f, s_b = retention(proj_c, log_f, log_b, zeros, zeros, rope=False)
    else:
        y_c, s_f, s_b = mixer(proj_c, proj_c, p, log_f, log_b, zeros, zeros, tables_c, False)
    hl = modulate(rmsnorm(xl, p['norm_mix']), sa_l, ca_l)
    y_l, _, _ = mixer(mm(hl, w_in), proj_c, p, log_f, log_b, s_f, s_b, tables_l, True)
    xl = xl + ga_l * y_l
    hf_l = modulate(rmsnorm(xl, p['norm_ffn']), sf_l, cf_l)
    xl = xl + gf_l * peer(hf_l, p['peer_wq'], p['peer_keys'], p['peer_u'], p['peer_v'])
    if last:
        return xl, None
    ga_c, sf_c, cf_c, gf_c = mod_c_parts[2], mod_c_parts[3], mod_c_parts[4], mod_c_parts[5]
    xc = xc + ga_c * y_c
    hf_c = modulate(rmsnorm(xc, p['norm_ffn']), sf_c, cf_c)
    xc = xc + gf_c * peer(hf_c, p['peer_wq'], p['peer_keys'], p['peer_u'], p['peer_v'])
    return xl, xc


def kernel(x, c, ctx, c_ctx, w_mod, b_mod, norm_mix, w_in, hy_conv_w, hy_conv_b,
           hy_f_w1, hy_f_b1, hy_f_freq, hy_f_w2, hy_f_b2, hy_f_w3, hy_f_b3, hy_f_w4,
           hy_skip, ret_decay, na_rpb, q_norm, k_norm, w_out, norm_ffn,
           peer_wq, peer_keys, peer_u, peer_v):
    xl, xc = x, ctx
    sc = jax.nn.silu(c)
    scc = jax.nn.silu(c_ctx)
    tables_l = dft_tables(x.shape[1])
    tables_c = dft_tables(ctx.shape[1])
    for l in range(DEPTH):
        last = l == DEPTH - 1
        p = {
            'norm_mix': norm_mix[l], 'w_in': w_in[l],
            'hy_conv_w': hy_conv_w[l], 'hy_conv_b': hy_conv_b[l],
            'hy_f': (hy_f_w1[l], hy_f_b1[l], hy_f_freq[l], hy_f_w2[l], hy_f_b2[l],
                     hy_f_w3[l], hy_f_b3[l], hy_f_w4[l]),
            'hy_skip': hy_skip[l], 'ret_decay': ret_decay[l], 'na_rpb': na_rpb[l],
            'q_norm': q_norm[l], 'k_norm': k_norm[l], 'w_out': w_out[l],
            'norm_ffn': norm_ffn[l], 'peer_wq': peer_wq[l], 'peer_keys': peer_keys[l],
            'peer_u': peer_u[l].astype(jnp.bfloat16), 'peer_v': peer_v[l].astype(jnp.bfloat16),
        }
        mod_l = (sc @ w_mod[l] + b_mod[l])[:, None, :]
        n_ctx = 2 if last else N_MOD
        mod_c = (scc @ w_mod[l][:, :n_ctx * D_MODEL] + b_mod[l][:n_ctx * D_MODEL])[None, None, :]
        xl, xc = layer(xl, xc, mod_l, mod_c, p, last, tables_l, tables_c)
    return xl
```

```python
import functools
import math

import jax
import jax.numpy as jnp
from jax import lax
from jax.experimental import pallas as pl
from jax.experimental.pallas import tpu as pltpu

D_MODEL = 4096
BATCH = 4
SEQ = 4096
DEPTH = 2

GRID_W = 64
CTX_LEN = 256
HEAD_DIM = 128
N_MIX_HEADS = D_MODEL // HEAD_DIM
HY_GROUPS = N_MIX_HEADS // 4
RET_HEADS = (N_MIX_HEADS - HY_GROUPS) // 2
NA_HEADS = N_MIX_HEADS - HY_GROUPS - RET_HEADS
HY_W = HY_GROUPS * HEAD_DIM
RET_W = RET_HEADS * HEAD_DIM
NA_W = NA_HEADS * HEAD_DIM
K_RK = 0
K_RV = K_RK + RET_W
K_NK = K_RV + RET_W
K_NV = K_NK + NA_W
KV_COLS = K_NV + NA_W
R_HY = 0
R_RQ = R_HY + 3 * HY_W
R_RG = R_RQ + RET_W
R_NQ = R_RG + RET_W
REST_COLS = R_NQ + NA_W
IN_COLS = KV_COLS + REST_COLS
N_MOD = 6
HY_ORDER = 2
HY_SHORT = 3
HY_EMB = 33
HY_BANDS = (HY_EMB - 1) // 2
HY_FILTER_HIDDEN = 64
HY_DECAY_TARGET = 1e-2
HY_SHORT_DECAY_PCT = 0.3
HY_LONG_DECAY_PCT = 1.5
HY_MAX_DECAY = math.log(HY_DECAY_TARGET) / HY_SHORT_DECAY_PCT
HY_MIN_DECAY = math.log(HY_DECAY_TARGET) / HY_LONG_DECAY_PCT
HY_FILTER_SCALE = 0.05
RET_CHUNK = 128
RET_SCALE = HEAD_DIM ** -0.5
ROPE_BASE = 10000.0
NA_KH = 8
NA_KW = 16
PEER_HEADS = 8
PEER_NKEYS = 128
PEER_N = PEER_NKEYS * PEER_NKEYS
PEER_TOPK = 16
PEER_DK = 256
PEER_BLOCK = 64
EPS = 1e-6
NEG_INF = -1e30

VMEM_LIMIT_BYTES = 56 * 1024 * 1024


def _mm_kernel(a_ref, b_ref, o_ref):
    o_ref[...] = jnp.dot(a_ref[...], b_ref[...], preferred_element_type=jnp.float32)


def _pick_block(n, candidates):
    for c in candidates:
        if n % c == 0:
            return c
    return n


def mm(a, b):
    lead = a.shape[:-1]
    K = a.shape[-1]
    a2 = a.reshape(-1, K).astype(jnp.bfloat16)
    b2 = b.astype(jnp.bfloat16)
    M, N = a2.shape[0], b2.shape[1]
    bm = _pick_block(M, (1024, 512, 256, 128, 64, 32, 16, 8))
    bn = _pick_block(N, (1024, 768, 512, 256, 128))
    out = pl.pallas_call(
        _mm_kernel,
        grid=(M // bm, N // bn),
        in_specs=[pl.BlockSpec((bm, K), lambda i, j: (i, 0)),
                  pl.BlockSpec((K, bn), lambda i, j: (0, j))],
        out_specs=pl.BlockSpec((bm, bn), lambda i, j: (i, j)),
        out_shape=jax.ShapeDtypeStruct((M, N), jnp.float32),
        compiler_params=pltpu.CompilerParams(
            dimension_semantics=("parallel", "parallel"),
            vmem_limit_bytes=VMEM_LIMIT_BYTES),
        name="mm",
    )(a2, b2)
    return out.reshape(lead + (N,))


def hyena_filters(L, w1, b1, freq, w2, b2, w3, b3, w4):
    f32 = lambda a: a.astype(jnp.float32)
    t = jnp.linspace(0.0, 1.0, L, dtype=jnp.float32)[:, None]
    w = 2.0 * math.pi * jnp.arange(L, dtype=jnp.float32)[:, None] / L
    f = jnp.linspace(1e-4, HY_BANDS - 1, HY_BANDS, dtype=jnp.float32)[None, :]
    z = jnp.concatenate([t, jnp.cos(f * w), -jnp.sin(f * w)], axis=-1)
    h = jnp.sin(f32(freq) * (z @ f32(w1) + f32(b1)))
    h = jnp.sin(f32(freq) * (h @ f32(w2) + f32(b2)))
    h = jnp.sin(f32(freq) * (h @ f32(w3) + f32(b3)))
    k = (h @ f32(w4)).reshape(L, HY_ORDER, 2, HY_W)
    deltas = jnp.abs(jnp.linspace(HY_MIN_DECAY, HY_MAX_DECAY, HY_W, dtype=jnp.float32))
    return k * jnp.exp(-t[:, :, None, None] * deltas)


def _mm_nt_kernel(w_ref, x_ref, o_ref):
    o_ref[...] = lax.dot_general(w_ref[...], x_ref[...], (((1,), (1,)), ((), ())),
                                 preferred_element_type=jnp.float32)


def mm_nt(w, x):
    N, K = w.shape
    M = x.shape[0]
    bn = _pick_block(N, (1024, 512, 256, 128))
    bm = _pick_block(M, (1024, 512, 256, 128))
    return pl.pallas_call(
        _mm_nt_kernel,
        grid=(M // bm, N // bn),
        in_specs=[pl.BlockSpec((bn, K), lambda i, j: (j, 0)),
                  pl.BlockSpec((bm, K), lambda i, j: (i, 0))],
        out_specs=pl.BlockSpec((bn, bm), lambda i, j: (j, i)),
        out_shape=jax.ShapeDtypeStruct((N, M), jnp.float32),
        compiler_params=pltpu.CompilerParams(
            dimension_semantics=("parallel", "parallel"),
            vmem_limit_bytes=VMEM_LIMIT_BYTES),
        name="mm_nt",
    )(w.astype(jnp.bfloat16), x.astype(jnp.bfloat16))


def _top_values(s, k):
    vals = []
    work = s
    for _ in range(k):
        m = jnp.max(work, axis=0, keepdims=True)
        vals.append(m)
        work = jnp.where(work == m, -jnp.inf, work)
    return jnp.concatenate(vals, axis=0)


def _peer_route_kernel(qt_ref, keys_ref, s1_ref, b_ref, thr_ref, a_ref):
    half = PEER_DK // 2
    K = PEER_TOPK
    for h in range(PEER_HEADS):
        s = []
        for p in range(2):
            q = qt_ref[pl.ds((2 * h + p) * half, half), :]
            s.append(jnp.dot(keys_ref[2 * h + p], q, preferred_element_type=jnp.float32,
                             precision=lax.Precision.HIGHEST))
        s0, s1 = s
        v0 = _top_values(s0, K)
        v1 = _top_values(s1, K)
        v0_tail = jnp.where(lax.broadcasted_iota(jnp.int32, v0.shape, 0) < 4, -jnp.inf, v0)
        cand = jnp.concatenate([v0[0:1] + v1] + [v0[a:a + 1] + v1[0:8] for a in (1, 2, 3)]
                               + [v1[b:b + 1] + v0_tail for b in (0, 1, 2)], axis=0)
        tau = _top_values(cand, K)[K - 1:K]
        m = v0[0:1] + v1[0:1]
        z = jnp.sum(jnp.where(cand >= tau, jnp.exp(cand - m), 0.0), axis=0, keepdims=True)
        thr = jnp.full_like(s0, jnp.inf)
        for b in range(K):
            vb = v1[b:b + 1]
            thr = jnp.minimum(thr, jnp.where(s0 + vb >= tau, vb, jnp.inf))
        s1_ref[h] = s1
        b_ref[h] = jnp.exp(s1 - v1[0:1])
        thr_ref[h] = thr
        a_ref[h] = jnp.exp(s0 - v0[0:1]) / z


PEER_ROUTE_LANES = 128


def peer_route(qt, sub_keys):
    T = qt.shape[1]
    keys = sub_keys.reshape(PEER_HEADS * 2, PEER_NKEYS, PEER_DK // 2).astype(jnp.float32)
    bt = PEER_ROUTE_LANES
    out = jax.ShapeDtypeStruct((PEER_HEADS, PEER_NKEYS, T), jnp.float32)
    ospec = pl.BlockSpec((PEER_HEADS, PEER_NKEYS, bt), lambda i: (0, 0, i))
    return pl.pallas_call(
        _peer_route_kernel,
        grid=(T // bt,),
        in_specs=[pl.BlockSpec((PEER_HEADS * PEER_DK, bt), lambda i: (0, i)),
                  pl.BlockSpec(keys.shape, lambda i: (0, 0, 0))],
        out_specs=[ospec] * 4,
        out_shape=[out] * 4,
        compiler_params=pltpu.CompilerParams(
            dimension_semantics=("parallel",), vmem_limit_bytes=VMEM_LIMIT_BYTES),
        name="peer_route",
    )(qt, keys)


PEER_DENSE_TOKENS = 512
PEER_DENSE_GROUPS = 4
PEER_LANE_CHUNK = 128


def _peer_dense_kernel(x_ref, u_ref, v_ref, s1_ref, b_ref, thr_ref, a_ref, o_ref, thr_s, a_s, *, groups):
    e = pl.program_id(1)

    @pl.when(e == 0)
    def _():
        o_ref[...] = jnp.zeros_like(o_ref)

    for g in range(groups):
        for h in range(PEER_HEADS):
            thr_s[g * PEER_HEADS + h] = thr_ref[h, pl.ds(e * groups + g, 1), :]
            a_s[g * PEER_HEADS + h] = a_ref[h, pl.ds(e * groups + g, 1), :]

    half_groups = groups // 2
    half_rows = half_groups * PEER_NKEYS
    pres = []
    for half in range(2):
        rows = pl.ds(half * half_rows, half_rows)
        pres.append(lax.dot_general(u_ref[rows, :], x_ref[...], (((1,), (1,)), ((), ())),
                                    preferred_element_type=jnp.float32))
    for half in range(2):
        acts = []
        for g in range(half_groups):
            chunks = []
            for c in range(x_ref.shape[0] // PEER_LANE_CHUNK):
                lanes = pl.ds(c * PEER_LANE_CHUNK, PEER_LANE_CHUNK)
                w = jnp.zeros((PEER_NKEYS, PEER_LANE_CHUNK), jnp.float32)
                for h in range(PEER_HEADS):
                    r = (half * half_groups + g) * PEER_HEADS + h
                    w = w + jnp.where(s1_ref[h, :, lanes] >= thr_s[r, :, lanes], b_ref[h, :, lanes], 0.0) * a_s[r, :, lanes]
                pre = pres[half][g * PEER_NKEYS:(g + 1) * PEER_NKEYS, c * PEER_LANE_CHUNK:(c + 1) * PEER_LANE_CHUNK]
                chunks.append((jax.nn.gelu(pre) * w).astype(jnp.bfloat16))
            acts.append(jnp.concatenate(chunks, axis=1))
        act = jnp.concatenate(acts, axis=0)
        rows = pl.ds(half * half_rows, half_rows)
        o_ref[...] += lax.dot_general(act, v_ref[rows, :], (((0,), (0,)), ((), ())),
                                      preferred_element_type=jnp.float32)


def peer_dense(x, u, v, s1, b, thr, a):
    T, D = x.shape
    bt = min(PEER_DENSE_TOKENS, T)
    groups = PEER_DENSE_GROUPS
    be = groups * PEER_NKEYS
    once = pl.Buffered(1)
    rspec = pl.BlockSpec((PEER_HEADS, PEER_NKEYS, bt), lambda t, e: (0, 0, t), pipeline_mode=once)
    return pl.pallas_call(
        functools.partial(_peer_dense_kernel, groups=groups),
        grid=(T // bt, PEER_N // be),
        in_specs=[pl.BlockSpec((bt, D), lambda t, e: (t, 0), pipeline_mode=once),
                  pl.BlockSpec((be, D), lambda t, e: (e, 0)),
                  pl.BlockSpec((be, D), lambda t, e: (e, 0)),
                  rspec, rspec, rspec, rspec],
        out_specs=pl.BlockSpec((bt, D), lambda t, e: (t, 0)),
        out_shape=jax.ShapeDtypeStruct((T, D), jnp.float32),
        scratch_shapes=[pltpu.VMEM((groups * PEER_HEADS, 1, bt), jnp.float32),
                        pltpu.VMEM((groups * PEER_HEADS, 1, bt), jnp.float32)],
        compiler_params=pltpu.CompilerParams(
            dimension_semantics=("parallel", "arbitrary"),
            vmem_limit_bytes=VMEM_LIMIT_BYTES),
        name="peer_dense",
    )(x, u, v, s1, b, thr, a)


def peer(h, w_q, sub_keys, exp_u, exp_v):
    B, L, D = h.shape
    x = h.reshape(B * L, D).astype(jnp.bfloat16)
    qt = mm_nt(w_q.T, x)
    s1, b, thr, a = peer_route(qt, sub_keys)
    return peer_dense(x, exp_u, exp_v, s1, b, thr, a).reshape(B, L, D)


def _col_spec(L, col0, width=HEAD_DIM):
    assert col0 % width == 0
    return pl.BlockSpec((1, L, width), lambda b, j: (b, 0, col0 // width + j))


def _bf16_dot(a, b, dims):
    return lax.dot_general(a.astype(jnp.bfloat16), b.astype(jnp.bfloat16), (dims, ((), ())),
                           preferred_element_type=jnp.float32)


_NN = ((1,), (0,))
_NT = ((1,), (1,))
_TN = ((0,), (0,))


def rope_tables(L):
    quarter = HEAD_DIM // 4
    t = jnp.arange(L)
    lane = jnp.arange(HEAD_DIM)
    pos = jnp.where(lane[None, :] < 2 * quarter, (t // GRID_W)[:, None], (t % GRID_W)[:, None]).astype(jnp.float32)
    inv = ROPE_BASE ** (-(lane % quarter).astype(jnp.float32) / quarter)
    ang = pos * inv[None, :]
    sign = jnp.where((lane % (2 * quarter)) < quarter, -1.0, 1.0)
    return jnp.cos(ang), jnp.sin(ang) * sign[None, :]


def _rope(x, cos, sin_signed):
    quarter = HEAD_DIM // 4
    lane = lax.broadcasted_iota(jnp.int32, x.shape, 1)
    partner = jnp.where((lane % (2 * quarter)) < quarter,
                        pltpu.roll(x, HEAD_DIM - quarter, axis=1), pltpu.roll(x, quarter, axis=1))
    return x * cos + partner * sin_signed


def _ret_kernel(*refs, n_chunks, rope):
    if rope:
        (q_ref, k_ref, v_ref, g_ref, sf0_ref, sb0_ref, dec_ref, cos_ref, sin_ref,
         o_ref, sf_ref, sb_ref, kvf, kvb, kr) = refs
    else:
        (q_ref, k_ref, v_ref, g_ref, sf0_ref, sb0_ref, dec_ref,
         o_ref, sf_ref, sb_ref, kvf, kvb, kr) = refs
    C = RET_CHUNK
    lf = dec_ref[0, 0:1, :]
    lb = dec_ref[0, 1:2, :]
    pos = lax.broadcasted_iota(jnp.int32, (C, HEAD_DIM), 0).astype(jnp.float32)
    kdf = jnp.exp(lf * (C - 1.0 - pos))
    kdb = jnp.exp(lb * pos)
    qdf = jnp.exp(lf * (pos + 1.0))
    qdb = jnp.exp(lb * (C - pos))
    cf = jnp.exp(lf * C)
    cb = jnp.exp(lb * C)
    diff = (lax.broadcasted_iota(jnp.int32, (C, C), 0) - lax.broadcasted_iota(jnp.int32, (C, C), 1)).astype(jnp.float32)
    dmat = (jnp.where(diff >= 0, jnp.exp(lf * jnp.maximum(diff, 0.0)), 0.0)
            + jnp.where(diff <= 0, jnp.exp(lb * jnp.maximum(-diff, 0.0)), 0.0))

    def rows(m):
        return pl.ds(pl.multiple_of(m * C, C), C)

    def summaries(m, carry):
        k = k_ref[0, rows(m), :] * RET_SCALE
        if rope:
            k = _rope(k, cos_ref[rows(m), :], sin_ref[rows(m), :])
        kr[rows(m), :] = k
        v = v_ref[0, rows(m), :]
        kvf[m] = _bf16_dot(k * kdf, v, _TN)
        kvb[m] = _bf16_dot(k * kdb, v, _TN)
        return carry

    lax.fori_loop(0, n_chunks, summaries, 0, unroll=4)

    def scan_f(m, s):
        nxt = s * cf + kvf[m]
        kvf[m] = s
        return nxt

    sf_ref[0, 0] = lax.fori_loop(0, n_chunks, scan_f, sf0_ref[0, 0])

    def scan_b(i, s):
        m = n_chunks - 1 - i
        nxt = s * cb + kvb[m]
        kvb[m] = s
        return nxt

    sb_ref[0, 0] = lax.fori_loop(0, n_chunks, scan_b, sb0_ref[0, 0])

    def outputs(m, carry):
        q = q_ref[0, rows(m), :]
        if rope:
            q = _rope(q, cos_ref[rows(m), :], sin_ref[rows(m), :])
        k = kr[rows(m), :]
        v = v_ref[0, rows(m), :]
        a = _bf16_dot(q, k, _NT) * dmat
        o = (_bf16_dot(a, v, _NN) + _bf16_dot(q * qdf, kvf[m], _NN) + _bf16_dot(q * qdb, kvb[m], _NN))
        o = o * lax.rsqrt(jnp.mean(o * o, axis=-1, keepdims=True) + EPS)
        o_ref[0, rows(m), :] = (o * jax.nn.silu(g_ref[0, rows(m), :])).astype(o_ref.dtype)
        return carry

    lax.fori_loop(0, n_chunks, outputs, 0, unroll=4)


def retention(proj, log_f, log_b, s_f, s_b, rope):
    B, L, _ = proj.shape
    n_chunks = L // RET_CHUNK
    dec = jnp.zeros((RET_HEADS, 8, HEAD_DIM), jnp.float32)
    dec = dec.at[:, 0, :].set(log_f[:, None]).at[:, 1, :].set(log_b[:, None])
    state_spec = pl.BlockSpec((1, 1, HEAD_DIM, HEAD_DIM), lambda b, h: (b, h, 0, 0))
    in_specs = [_col_spec(L, KV_COLS + R_RQ), _col_spec(L, K_RK), _col_spec(L, K_RV), _col_spec(L, KV_COLS + R_RG),
                state_spec, state_spec, pl.BlockSpec((1, 8, HEAD_DIM), lambda b, h: (h, 0, 0))]
    args = [proj, proj, proj, proj, s_f, s_b, dec]
    if rope:
        tab_spec = pl.BlockSpec((L, HEAD_DIM), lambda b, h: (0, 0))
        in_specs += [tab_spec, tab_spec]
        args += list(rope_tables(L))
    state_shape = jax.ShapeDtypeStruct((B, RET_HEADS, HEAD_DIM, HEAD_DIM), jnp.float32)
    return pl.pallas_call(
        functools.partial(_ret_kernel, n_chunks=n_chunks, rope=rope),
        grid=(B, RET_HEADS),
        in_specs=in_specs,
        out_specs=[_col_spec(L, 0), state_spec, state_spec],
        out_shape=[jax.ShapeDtypeStruct((B, L, RET_W), jnp.bfloat16), state_shape, state_shape],
        scratch_shapes=[pltpu.VMEM((n_chunks, HEAD_DIM, HEAD_DIM), jnp.float32),
                        pltpu.VMEM((n_chunks, HEAD_DIM, HEAD_DIM), jnp.float32),
                        pltpu.VMEM((L, HEAD_DIM), jnp.float32)],
        compiler_params=pltpu.CompilerParams(
            dimension_semantics=("parallel", "parallel"), vmem_limit_bytes=VMEM_LIMIT_BYTES),
        name="retention",
    )(*args)


def na_bias_windows(rpb):
    col = jnp.arange(GRID_W)
    cstart = jnp.clip(col - NA_KW // 2, 0, GRID_W - NA_KW)
    cmask = (col[None, :] >= cstart[:, None]) & (col[None, :] < cstart[:, None] + NA_KW)
    cidx = jnp.clip(col[None, :] - col[:, None] + NA_KW - 1, 0, 2 * NA_KW - 2)
    rpb_c = jnp.where(cmask[None, None], rpb[:, :, cidx].astype(jnp.float32), NEG_INF)
    wins = [jnp.concatenate([rpb_c[:, w + i] for i in range(NA_KH)], axis=-1) for w in range(NA_KH)]
    return jnp.stack(wins, axis=1)


def _na_kernel(*refs, n_rows, local):
    if local:
        (q_ref, k_ref, v_ref, kc_ref, vc_ref, bw_ref, qn_ref, kn_ref, o_ref, k_s, v_s, kc_s, vc_s) = refs
    else:
        (q_ref, kc_ref, vc_ref, qn_ref, kn_ref, o_ref, kc_s, vc_s) = refs
    scale = HEAD_DIM ** -0.5

    def norm(x, g_ref):
        return x * lax.rsqrt(jnp.mean(x * x, axis=-1, keepdims=True) + EPS) * g_ref[...]

    kc_s[...] = norm(kc_ref[0], kn_ref).astype(jnp.bfloat16)
    vc_s[...] = vc_ref[0].astype(jnp.bfloat16)
    if local:
        k_s[...] = norm(k_ref[0], kn_ref).astype(jnp.bfloat16)
        v_s[...] = v_ref[0].astype(jnp.bfloat16)

    def row(r, carry):
        qrows = pl.ds(pl.multiple_of(r * GRID_W, GRID_W), GRID_W)
        q = norm(q_ref[0, qrows, :], qn_ref).astype(jnp.bfloat16)
        s_ctx = _bf16_dot(q, kc_s[...], _NT) * scale
        m = jnp.max(s_ctx, axis=-1, keepdims=True)
        if local:
            start = jnp.clip(r - NA_KH // 2, 0, n_rows - NA_KH)
            krows = pl.ds(pl.multiple_of(start * GRID_W, GRID_W), NA_KH * GRID_W)
            s_loc = _bf16_dot(q, k_s[krows, :], _NT) * scale + bw_ref[0, start - r + NA_KH - 1]
            m = jnp.maximum(m, jnp.max(s_loc, axis=-1, keepdims=True))
            p_loc = jnp.exp(s_loc - m)
        p_ctx = jnp.exp(s_ctx - m)
        den = jnp.sum(p_ctx, axis=-1, keepdims=True)
        o = _bf16_dot(p_ctx, vc_s[...], _NN)
        if local:
            den = den + jnp.sum(p_loc, axis=-1, keepdims=True)
            o = o + _bf16_dot(p_loc, v_s[krows, :], _NN)
        o_ref[0, qrows, :] = (o / den).astype(o_ref.dtype)
        return carry

    lax.fori_loop(0, n_rows, row, 0, unroll=4)


def neighbourhood_attention(proj, proj_c, rpb, q_norm, k_norm, local):
    B, L, _ = proj.shape
    Lc = proj_c.shape[1]
    n_rows = L // GRID_W
    norm_spec = pl.BlockSpec((1, HEAD_DIM), lambda b, h: (0, 0))
    qn = q_norm.reshape(1, HEAD_DIM).astype(jnp.float32)
    kn = k_norm.reshape(1, HEAD_DIM).astype(jnp.float32)
    ctx_specs = [_col_spec(Lc, K_NK), _col_spec(Lc, K_NV)]
    ctx_scratch = [pltpu.VMEM((Lc, HEAD_DIM), jnp.bfloat16), pltpu.VMEM((Lc, HEAD_DIM), jnp.bfloat16)]
    if local:
        bw = na_bias_windows(rpb)
        in_specs = ([_col_spec(L, KV_COLS + R_NQ), _col_spec(L, K_NK), _col_spec(L, K_NV)] + ctx_specs
                    + [pl.BlockSpec((1, NA_KH, GRID_W, NA_KH * GRID_W), lambda b, h: (h, 0, 0, 0)), norm_spec, norm_spec])
        args = [proj, proj, proj, proj_c, proj_c, bw, qn, kn]
        scratch = [pltpu.VMEM((L, HEAD_DIM), jnp.bfloat16), pltpu.VMEM((L, HEAD_DIM), jnp.bfloat16)] + ctx_scratch
    else:
        in_specs = [_col_spec(L, KV_COLS + R_NQ)] + ctx_specs + [norm_spec, norm_spec]
        args = [proj, proj_c, proj_c, qn, kn]
        scratch = ctx_scratch
    return pl.pallas_call(
        functools.partial(_na_kernel, n_rows=n_rows, local=local),
        grid=(B, NA_HEADS),
        in_specs=in_specs,
        out_specs=_col_spec(L, 0),
        out_shape=jax.ShapeDtypeStruct((B, L, NA_W), jnp.bfloat16),
        scratch_shapes=scratch,
        compiler_params=pltpu.CompilerParams(
            dimension_semantics=("parallel", "parallel"), vmem_limit_bytes=VMEM_LIMIT_BYTES),
        name="neighbourhood_attention",
    )(*args)


def dft_tables(L):
    n = 2 * L
    f = jnp.arange(L, dtype=jnp.int32)[:, None]
    t = jnp.arange(L, dtype=jnp.int32)[None, :]
    ang = ((f * t) % n).astype(jnp.float32) * (2.0 * math.pi / n)
    cm = jnp.cos(ang)
    nyquist = jnp.where(t % 2 == 0, 1.0, -1.0)
    sm = jnp.where(f == 0, nyquist, jnp.sin(ang))
    w = jnp.where(f == 0, 1.0, 2.0) / n
    bf = jnp.bfloat16
    return cm.astype(bf), sm.astype(bf), (cm * w).T.astype(bf), (sm * w).T.astype(bf)


def hy_filter_spectrum(k, cm, sm):
    L = k.shape[0]
    kf = k[:, :, 0]
    kb = k[:, :, 1].at[0].set(0.0)
    ksum = (kf + kb).reshape(L, -1)
    kdiff = (kf - kb).reshape(L, -1)
    kc = mm(cm, ksum)
    ks = mm(sm, kdiff)
    sign = jnp.where(jnp.arange(L) % 2 == 0, 1.0, -1.0)[:, None]
    ks = ks.at[0].set(jnp.sum(sign * ksum, axis=0))
    split = lambda a: jnp.moveaxis(a.reshape(L, HY_ORDER, -1), 1, 0)
    return split(kc), split(ks)


def _hy_short_kernel(u_ref, w_ref, b_ref, o_ref):
    u = u_ref[0]
    L = u.shape[0]
    t = lax.broadcasted_iota(jnp.int32, u.shape, 0)
    prev = jnp.where(t == 0, 0.0, pltpu.roll(u, 1, axis=0))
    nxt = jnp.where(t == L - 1, 0.0, pltpu.roll(u, L - 1, axis=0))
    o_ref[0] = prev * w_ref[0:1, :] + u * w_ref[1:2, :] + nxt * w_ref[2:3, :] + b_ref[...]


HY_SHORT_COLS = 256


def hy_short(proj, conv_w, conv_b):
    B, L, _ = proj.shape
    cols = HY_SHORT_COLS
    n = 3 * HY_W
    return pl.pallas_call(
        _hy_short_kernel,
        grid=(B, n // cols),
        in_specs=[_col_spec(L, KV_COLS + R_HY, cols),
                  pl.BlockSpec((HY_SHORT, cols), lambda b, j: (0, j)),
                  pl.BlockSpec((1, cols), lambda b, j: (0, j))],
        out_specs=_col_spec(L, 0, cols),
        out_shape=jax.ShapeDtypeStruct((B, L, n), jnp.float32),
        compiler_params=pltpu.CompilerParams(
            dimension_semantics=("parallel", "parallel"), vmem_limit_bytes=VMEM_LIMIT_BYTES),
        name="hy_short",
    )(proj, conv_w.astype(jnp.float32), conv_b.reshape(1, n).astype(jnp.float32))


def _hy_fwd_kernel(cm_ref, sm_ref, u_ref, kc_ref, ks_ref, yc_ref, ys_ref):
    u = u_ref[0]
    uc = jnp.dot(cm_ref[...], u, preferred_element_type=jnp.float32)
    us = jnp.dot(sm_ref[...], u, preferred_element_type=jnp.float32)
    kc = kc_ref[...]
    ks = ks_ref[...]
    fb = uc.shape[0]
    f = pl.program_id(0) * fb + lax.broadcasted_iota(jnp.int32, (fb, 1), 0)
    packed = f == 0
    yc_ref[0] = jnp.where(packed, uc * kc, uc * kc - us * ks).astype(yc_ref.dtype)
    ys_ref[0] = jnp.where(packed, us * ks, uc * ks + us * kc).astype(ys_ref.dtype)


HY_DFT_ROWS = 512


def hy_fwd(u, cm, sm, kc, ks):
    B, L, C = u.shape
    fb = min(HY_DFT_ROWS, L)
    mat_spec = pl.BlockSpec((fb, L), lambda f, b: (f, 0))
    k_spec = pl.BlockSpec((fb, C), lambda f, b: (f, 0))
    y_spec = pl.BlockSpec((1, fb, C), lambda f, b: (b, f, 0))
    y_shape = jax.ShapeDtypeStruct((B, L, C), jnp.bfloat16)
    return pl.pallas_call(
        _hy_fwd_kernel,
        grid=(L // fb, B),
        in_specs=[mat_spec, mat_spec, pl.BlockSpec((1, L, C), lambda f, b: (b, 0, 0)), k_spec, k_spec],
        out_specs=[y_spec, y_spec],
        out_shape=[y_shape, y_shape],
        compiler_params=pltpu.CompilerParams(
            dimension_semantics=("parallel", "parallel"), vmem_limit_bytes=VMEM_LIMIT_BYTES),
        name="hy_fwd",
    )(cm, sm, u, kc, ks)


def _hy_inv_kernel(gc_ref, gs_ref, yc_ref, ys_ref, u_ref, gate_ref, skip_ref, o_ref):
    conv = (jnp.dot(gc_ref[...], yc_ref[0], preferred_element_type=jnp.float32)
            + jnp.dot(gs_ref[...], ys_ref[0], preferred_element_type=jnp.float32))
    o_ref[0] = (gate_ref[0] * (conv + u_ref[0] * skip_ref[...])).astype(o_ref.dtype)


HY_INV_COLS = 512


def hy_inv(yc, ys, gc, gs, u, u_col0, gate, gate_col0, skip, out_dtype):
    B, L, C = yc.shape
    tb = min(HY_DFT_ROWS, L)
    cols = HY_INV_COLS
    once = pl.Buffered(1)
    mat_spec = pl.BlockSpec((tb, L), lambda b, j, t: (t, 0))
    y_spec = pl.BlockSpec((1, L, cols), lambda b, j, t: (b, 0, j), pipeline_mode=once)

    def win(col0):
        return pl.BlockSpec((1, tb, cols), lambda b, j, t: (b, t, col0 // cols + j))

    return pl.pallas_call(
        _hy_inv_kernel,
        grid=(B, C // cols, L // tb),
        in_specs=[mat_spec, mat_spec, y_spec, y_spec, win(u_col0), win(gate_col0),
                  pl.BlockSpec((1, cols), lambda b, j, t: (0, j))],
        out_specs=win(0),
        out_shape=jax.ShapeDtypeStruct((B, L, C), out_dtype),
        compiler_params=pltpu.CompilerParams(
            dimension_semantics=("parallel", "parallel", "arbitrary"), vmem_limit_bytes=VMEM_LIMIT_BYTES),
        name="hy_inv",
    )(gc, gs, yc, ys, u, gate, skip.reshape(1, C).astype(jnp.float32))


def hyena(proj, conv_w, conv_b, filt, skip, tables):
    B, L, _ = proj.shape
    cm, sm, gc, gs = tables
    hyu = hy_short(proj, conv_w, conv_b)
    kc, ks = hy_filter_spectrum(hyena_filters(L, *filt), cm, sm)
    v = hyu[..., :HY_W].astype(jnp.bfloat16)
    yc, ys = hy_fwd(v, cm, sm, kc[0], ks[0])
    z1 = hy_inv(yc, ys, gc, gs, hyu, 0, hyu, HY_W, skip[0], jnp.float32)
    yc, ys = hy_fwd(z1.astype(jnp.bfloat16), cm, sm, kc[1], ks[1])
    return hy_inv(yc, ys, gc, gs, z1, 0, hyu, 2 * HY_W, skip[1], jnp.bfloat16)


def mixer(proj, proj_c, p, log_f, log_b, s_f, s_b, tables, latent):
    hy = hyena(proj, p['hy_conv_w'], p['hy_conv_b'], p['hy_f'], p['hy_skip'], tables)
    ro, sf, sb = retention(proj, log_f, log_b, s_f, s_b, rope=latent)
    no = neighbourhood_attention(proj, proj_c, p['na_rpb'], p['q_norm'], p['k_norm'], local=latent)
    return mm(jnp.concatenate([hy, ro, no], axis=-1), p['w_out']), sf, sb


def _norm_mod_kernel(*refs, residual):
    if residual:
        x_ref, g_ref, shift_ref, scale_ref, res_ref, gate_ref, xo_ref, h_ref = refs
        x = x_ref[0] + gate_ref[0] * res_ref[0]
        xo_ref[0] = x
    else:
        x_ref, g_ref, shift_ref, scale_ref, h_ref = refs
        x = x_ref[0]
    y = x * lax.rsqrt(jnp.mean(x * x, axis=-1, keepdims=True) + EPS) * g_ref[...]
    h_ref[0] = (y * (1.0 + scale_ref[0]) + shift_ref[0]).astype(h_ref.dtype)


NORM_ROWS = 256


def norm_mod(x, g, shift, scale, res=None, gate=None):
    B, L, D = x.shape
    rb = min(NORM_ROWS, L)
    residual = res is not None

    def vec_spec(v):
        per_batch = v.shape[0] > 1
        return pl.BlockSpec((1, 1, D), lambda b, r: (b if per_batch else 0, 0, 0))

    row_spec = pl.BlockSpec((1, rb, D), lambda b, r: (b, r, 0))
    in_specs = [row_spec, pl.BlockSpec((1, D), lambda b, r: (0, 0)), vec_spec(shift), vec_spec(scale)]
    args = [x, g.reshape(1, D).astype(jnp.float32), shift, scale]
    h_shape = jax.ShapeDtypeStruct((B, L, D), jnp.bfloat16)
    if residual:
        in_specs += [row_spec, vec_spec(gate)]
        args += [res, gate]
        out_specs, out_shape = [row_spec, row_spec], [jax.ShapeDtypeStruct((B, L, D), jnp.float32), h_shape]
    else:
        out_specs, out_shape = row_spec, h_shape
    return pl.pallas_call(
        functools.partial(_norm_mod_kernel, residual=residual),
        grid=(B, L // rb),
        in_specs=in_specs,
        out_specs=out_specs,
        out_shape=out_shape,
        compiler_params=pltpu.CompilerParams(
            dimension_semantics=("parallel", "parallel"), vmem_limit_bytes=VMEM_LIMIT_BYTES),
        name="norm_mod",
    )(*args)


def layer(xl, xc, mod_l, mod_c, p, last, tables_l, tables_c):
    sa_l, ca_l, ga_l, sf_l, cf_l, gf_l = jnp.split(mod_l, N_MOD, axis=-1)
    mod_c_parts = jnp.split(mod_c, mod_c.shape[-1] // D_MODEL, axis=-1)
    sa_c, ca_c = mod_c_parts[0], mod_c_parts[1]
    log_f = -jnp.exp(p['ret_decay'][0].astype(jnp.float32))
    log_b = -jnp.exp(p['ret_decay'][1].astype(jnp.float32))
    w_in = p['w_in'].astype(jnp.bfloat16)
    hc = norm_mod(xc, p['norm_mix'], sa_c, ca_c)
    proj_c = mm(hc, w_in)
    zeros = jnp.zeros((xc.shape[0], RET_HEADS, HEAD_DIM, HEAD_DIM), jnp.float32)
    if last:
        _, s_f, s_b = retention(proj_c, log_f, log_b, zeros, zeros, rope=False)
    else:
        y_c, s_f, s_b = mixer(proj_c, proj_c, p, log_f, log_b, zeros, zeros, tables_c, False)
    hl = norm_mod(xl, p['norm_mix'], sa_l, ca_l)
    y_l, _, _ = mixer(mm(hl, w_in), proj_c, p, log_f, log_b, s_f, s_b, tables_l, True)
    xl, hf_l = norm_mod(xl, p['norm_ffn'], sf_l, cf_l, y_l, ga_l)
    xl = xl + gf_l * peer(hf_l, p['peer_wq'], p['peer_keys'], p['peer_u'], p['peer_v'])
    if last:
        return xl, None
    ga_c, sf_c, cf_c, gf_c = mod_c_parts[2], mod_c_parts[3], mod_c_parts[4], mod_c_parts[5]
    xc, hf_c = norm_mod(xc, p['norm_ffn'], sf_c, cf_c, y_c, ga_c)
    xc = xc + gf_c * peer(hf_c, p['peer_wq'], p['peer_keys'], p['peer_u'], p['peer_v'])
    return xl, xc


def kernel(x, c, ctx, c_ctx, w_mod, b_mod, norm_mix, w_in, hy_conv_w, hy_conv_b,
           hy_f_w1, hy_f_b1, hy_f_freq, hy_f_w2, hy_f_b2, hy_f_w3, hy_f_b3, hy_f_w4,
           hy_skip, ret_decay, na_rpb, q_norm, k_norm, w_out, norm_ffn,
           peer_wq, peer_keys, peer_u, peer_v):
    xl, xc = x, ctx
    sc = jax.nn.silu(c)
    scc = jax.nn.silu(c_ctx)
    tables_l = dft_tables(x.shape[1])
    tables_c = dft_tables(ctx.shape[1])
    for l in range(DEPTH):
        last = l == DEPTH - 1
        p = {
            'norm_mix': norm_mix[l], 'w_in': w_in[l],
            'hy_conv_w': hy_conv_w[l], 'hy_conv_b': hy_conv_b[l],
            'hy_f': (hy_f_w1[l], hy_f_b1[l], hy_f_freq[l], hy_f_w2[l], hy_f_b2[l],
                     hy_f_w3[l], hy_f_b3[l], hy_f_w4[l]),
            'hy_skip': hy_skip[l], 'ret_decay': ret_decay[l], 'na_rpb': na_rpb[l],
            'q_norm': q_norm[l], 'k_norm': k_norm[l], 'w_out': w_out[l],
            'norm_ffn': norm_ffn[l], 'peer_wq': peer_wq[l], 'peer_keys': peer_keys[l],
            'peer_u': peer_u[l].astype(jnp.bfloat16), 'peer_v': peer_v[l].astype(jnp.bfloat16),
        }
        mod_l = (sc @ w_mod[l] + b_mod[l])[:, None, :]
        n_ctx = 2 if last else N_MOD
        mod_c = (scc @ w_mod[l][:, :n_ctx * D_MODEL] + b_mod[l][:n_ctx * D_MODEL])[None, None, :]
        xl, xc = layer(xl, xc, mod_l, mod_c, p, last, tables_l, tables_c)
    return xl
```

```python
import functools
import math

import jax
import jax.numpy as jnp
from jax import lax
from jax.experimental import pallas as pl
from jax.experimental.pallas import tpu as pltpu

D_MODEL = 4096
BATCH = 4
SEQ = 4096
DEPTH = 2

GRID_W = 64
CTX_LEN = 256
HEAD_DIM = 128
N_MIX_HEADS = D_MODEL // HEAD_DIM
HY_GROUPS = N_MIX_HEADS // 4
RET_HEADS = (N_MIX_HEADS - HY_GROUPS) // 2
NA_HEADS = N_MIX_HEADS - HY_GROUPS - RET_HEADS
HY_W = HY_GROUPS * HEAD_DIM
RET_W = RET_HEADS * HEAD_DIM
NA_W = NA_HEADS * HEAD_DIM
K_RK = 0
K_RV = K_RK + RET_W
K_NK = K_RV + RET_W
K_NV = K_NK + NA_W
KV_COLS = K_NV + NA_W
R_HY = 0
R_RQ = R_HY + 3 * HY_W
R_RG = R_RQ + RET_W
R_NQ = R_RG + RET_W
REST_COLS = R_NQ + NA_W
IN_COLS = KV_COLS + REST_COLS
N_MOD = 6
HY_ORDER = 2
HY_SHORT = 3
HY_EMB = 33
HY_BANDS = (HY_EMB - 1) // 2
HY_FILTER_HIDDEN = 64
HY_DECAY_TARGET = 1e-2
HY_SHORT_DECAY_PCT = 0.3
HY_LONG_DECAY_PCT = 1.5
HY_MAX_DECAY = math.log(HY_DECAY_TARGET) / HY_SHORT_DECAY_PCT
HY_MIN_DECAY = math.log(HY_DECAY_TARGET) / HY_LONG_DECAY_PCT
HY_FILTER_SCALE = 0.05
RET_CHUNK = 128
RET_SCALE = HEAD_DIM ** -0.5
ROPE_BASE = 10000.0
NA_KH = 8
NA_KW = 16
PEER_HEADS = 8
PEER_NKEYS = 128
PEER_N = PEER_NKEYS * PEER_NKEYS
PEER_TOPK = 16
PEER_DK = 256
PEER_BLOCK = 64
EPS = 1e-6
NEG_INF = -1e30

VMEM_LIMIT_BYTES = 56 * 1024 * 1024


def _mm_kernel(a_ref, b_ref, o_ref):
    o_ref[...] = jnp.dot(a_ref[...], b_ref[...], preferred_element_type=jnp.float32)


def _pick_block(n, candidates):
    for c in candidates:
        if n % c == 0:
            return c
    return n


def mm(a, b):
    lead = a.shape[:-1]
    K = a.shape[-1]
    a2 = a.reshape(-1, K).astype(jnp.bfloat16)
    b2 = b.astype(jnp.bfloat16)
    M, N = a2.shape[0], b2.shape[1]
    bm = _pick_block(M, (1024, 512, 256, 128, 64, 32, 16, 8))
    bn = _pick_block(N, (1024, 768, 512, 256, 128))
    out = pl.pallas_call(
        _mm_kernel,
        grid=(M // bm, N // bn),
        in_specs=[pl.BlockSpec((bm, K), lambda i, j: (i, 0)),
                  pl.BlockSpec((K, bn), lambda i, j: (0, j))],
        out_specs=pl.BlockSpec((bm, bn), lambda i, j: (i, j)),
        out_shape=jax.ShapeDtypeStruct((M, N), jnp.float32),
        compiler_params=pltpu.CompilerParams(
            dimension_semantics=("parallel", "parallel"),
            vmem_limit_bytes=VMEM_LIMIT_BYTES),
        name="mm",
    )(a2, b2)
    return out.reshape(lead + (N,))


def hyena_filters(L, w1, b1, freq, w2, b2, w3, b3, w4):
    f32 = lambda a: a.astype(jnp.float32)
    t = jnp.linspace(0.0, 1.0, L, dtype=jnp.float32)[:, None]
    w = 2.0 * math.pi * jnp.arange(L, dtype=jnp.float32)[:, None] / L
    f = jnp.linspace(1e-4, HY_BANDS - 1, HY_BANDS, dtype=jnp.float32)[None, :]
    z = jnp.concatenate([t, jnp.cos(f * w), -jnp.sin(f * w)], axis=-1)
    h = jnp.sin(f32(freq) * (z @ f32(w1) + f32(b1)))
    h = jnp.sin(f32(freq) * (h @ f32(w2) + f32(b2)))
    h = jnp.sin(f32(freq) * (h @ f32(w3) + f32(b3)))
    k = (h @ f32(w4)).reshape(L, HY_ORDER, 2, HY_W)
    deltas = jnp.abs(jnp.linspace(HY_MIN_DECAY, HY_MAX_DECAY, HY_W, dtype=jnp.float32))
    return k * jnp.exp(-t[:, :, None, None] * deltas)


def _mm_nt_kernel(w_ref, x_ref, o_ref):
    o_ref[...] = lax.dot_general(w_ref[...], x_ref[...], (((1,), (1,)), ((), ())),
                                 preferred_element_type=jnp.float32)


def mm_nt(w, x):
    N, K = w.shape
    M = x.shape[0]
    bn = _pick_block(N, (1024, 512, 256, 128))
    bm = _pick_block(M, (1024, 512, 256, 128))
    return pl.pallas_call(
        _mm_nt_kernel,
        grid=(M // bm, N // bn),
        in_specs=[pl.BlockSpec((bn, K), lambda i, j: (j, 0)),
                  pl.BlockSpec((bm, K), lambda i, j: (i, 0))],
        out_specs=pl.BlockSpec((bn, bm), lambda i, j: (j, i)),
        out_shape=jax.ShapeDtypeStruct((N, M), jnp.float32),
        compiler_params=pltpu.CompilerParams(
            dimension_semantics=("parallel", "parallel"),
            vmem_limit_bytes=VMEM_LIMIT_BYTES),
        name="mm_nt",
    )(w.astype(jnp.bfloat16), x.astype(jnp.bfloat16))


def _top_values(s, k):
    vals = []
    work = s
    for _ in range(k):
        m = jnp.max(work, axis=0, keepdims=True)
        vals.append(m)
        work = jnp.where(work == m, -jnp.inf, work)
    return jnp.concatenate(vals, axis=0)


def _peer_route_kernel(qt_ref, keys_ref, s1_ref, b_ref, thr_ref, a_ref):
    half = PEER_DK // 2
    K = PEER_TOPK
    for h in range(PEER_HEADS):
        s = []
        for p in range(2):
            q = qt_ref[pl.ds((2 * h + p) * half, half), :]
            s.append(jnp.dot(keys_ref[2 * h + p], q, preferred_element_type=jnp.float32,
                             precision=lax.Precision.HIGHEST))
        s0, s1 = s
        v0 = _top_values(s0, K)
        v1 = _top_values(s1, K)
        v0_tail = jnp.where(lax.broadcasted_iota(jnp.int32, v0.shape, 0) < 4, -jnp.inf, v0)
        cand = jnp.concatenate([v0[0:1] + v1] + [v0[a:a + 1] + v1[0:8] for a in (1, 2, 3)]
                               + [v1[b:b + 1] + v0_tail for b in (0, 1, 2)], axis=0)
        tau = _top_values(cand, K)[K - 1:K]
        m = v0[0:1] + v1[0:1]
        z = jnp.sum(jnp.where(cand >= tau, jnp.exp(cand - m), 0.0), axis=0, keepdims=True)
        thr = jnp.full_like(s0, jnp.inf)
        for b in range(K):
            vb = v1[b:b + 1]
            thr = jnp.minimum(thr, jnp.where(s0 + vb >= tau, vb, jnp.inf))
        s1_ref[h] = s1
        b_ref[h] = jnp.exp(s1 - v1[0:1])
        thr_ref[h] = thr
        a_ref[h] = jnp.exp(s0 - v0[0:1]) / z


PEER_ROUTE_LANES = 128


def peer_route(qt, sub_keys):
    T = qt.shape[1]
    keys = sub_keys.reshape(PEER_HEADS * 2, PEER_NKEYS, PEER_DK // 2).astype(jnp.float32)
    bt = PEER_ROUTE_LANES
    out = jax.ShapeDtypeStruct((PEER_HEADS, PEER_NKEYS, T), jnp.float32)
    ospec = pl.BlockSpec((PEER_HEADS, PEER_NKEYS, bt), lambda i: (0, 0, i))
    return pl.pallas_call(
        _peer_route_kernel,
        grid=(T // bt,),
        in_specs=[pl.BlockSpec((PEER_HEADS * PEER_DK, bt), lambda i: (0, i)),
                  pl.BlockSpec(keys.shape, lambda i: (0, 0, 0))],
        out_specs=[ospec] * 4,
        out_shape=[out] * 4,
        compiler_params=pltpu.CompilerParams(
            dimension_semantics=("parallel",), vmem_limit_bytes=VMEM_LIMIT_BYTES),
        name="peer_route",
    )(qt, keys)


PEER_DENSE_TOKENS = 512
PEER_DENSE_GROUPS = 4
PEER_LANE_CHUNK = 128
PEER_SPLIT_V = False


def _peer_dense_kernel(x_ref, u_ref, v_ref, s1_ref, b_ref, thr_ref, a_ref, o_ref, thr_s, a_s, pre_s, act_s, *, groups):
    e = pl.program_id(1)

    @pl.when(e == 0)
    def _():
        o_ref[...] = jnp.zeros_like(o_ref)

    for g in range(groups):
        for h in range(PEER_HEADS):
            thr_s[g * PEER_HEADS + h] = thr_ref[h, pl.ds(e * groups + g, 1), :]
            a_s[g * PEER_HEADS + h] = a_ref[h, pl.ds(e * groups + g, 1), :]

    half_groups = groups // 2
    half_rows = half_groups * PEER_NKEYS
    for half in range(2):
        rows = pl.ds(half * half_rows, half_rows)
        pre_s[rows, :] = lax.dot_general(u_ref[rows, :], x_ref[...], (((1,), (1,)), ((), ())),
                                         preferred_element_type=jnp.float32)
    for half in range(2):
        for g in range(half_groups):
            gi = half * half_groups + g
            rows_g = pl.ds(gi * PEER_NKEYS, PEER_NKEYS)
            for c in range(x_ref.shape[0] // PEER_LANE_CHUNK):
                lanes = pl.ds(c * PEER_LANE_CHUNK, PEER_LANE_CHUNK)
                w = jnp.zeros((PEER_NKEYS, PEER_LANE_CHUNK), jnp.float32)
                for h in range(PEER_HEADS):
                    r = gi * PEER_HEADS + h
                    w = w + jnp.where(s1_ref[h, :, lanes] >= thr_s[r, :, lanes], b_ref[h, :, lanes], 0.0) * a_s[r, :, lanes]
                act_s[rows_g, lanes] = (jax.nn.gelu(pre_s[rows_g, lanes]) * w).astype(jnp.bfloat16)
        if PEER_SPLIT_V:
            rows = pl.ds(half * half_rows, half_rows)
            o_ref[...] += lax.dot_general(act_s[rows, :], v_ref[rows, :], (((0,), (0,)), ((), ())),
                                          preferred_element_type=jnp.float32)
    if not PEER_SPLIT_V:
        o_ref[...] += lax.dot_general(act_s[...], v_ref[...], (((0,), (0,)), ((), ())),
                                      preferred_element_type=jnp.float32)


def peer_dense(x, u, v, s1, b, thr, a):
    T, D = x.shape
    bt = min(PEER_DENSE_TOKENS, T)
    groups = PEER_DENSE_GROUPS
    be = groups * PEER_NKEYS
    once = pl.Buffered(1)
    rspec = pl.BlockSpec((PEER_HEADS, PEER_NKEYS, bt), lambda t, e: (0, 0, t), pipeline_mode=once)
    return pl.pallas_call(
        functools.partial(_peer_dense_kernel, groups=groups),
        grid=(T // bt, PEER_N // be),
        in_specs=[pl.BlockSpec((bt, D), lambda t, e: (t, 0), pipeline_mode=once),
                  pl.BlockSpec((be, D), lambda t, e: (e, 0)),
                  pl.BlockSpec((be, D), lambda t, e: (e, 0)),
                  rspec, rspec, rspec, rspec],
        out_specs=pl.BlockSpec((bt, D), lambda t, e: (t, 0)),
        out_shape=jax.ShapeDtypeStruct((T, D), jnp.float32),
        scratch_shapes=[pltpu.VMEM((groups * PEER_HEADS, 1, bt), jnp.float32),
                        pltpu.VMEM((groups * PEER_HEADS, 1, bt), jnp.float32),
                        pltpu.VMEM((be, bt), jnp.float32),
                        pltpu.VMEM((be, bt), jnp.bfloat16)],
        compiler_params=pltpu.CompilerParams(
            dimension_semantics=("parallel", "arbitrary"),
            vmem_limit_bytes=VMEM_LIMIT_BYTES),
        name="peer_dense",
    )(x, u, v, s1, b, thr, a)


def peer(h, w_q, sub_keys, exp_u, exp_v):
    B, L, D = h.shape
    x = h.reshape(B * L, D).astype(jnp.bfloat16)
    qt = mm_nt(w_q.T, x)
    s1, b, thr, a = peer_route(qt, sub_keys)
    return peer_dense(x, exp_u, exp_v, s1, b, thr, a).reshape(B, L, D)


def _col_spec(L, col0, width=HEAD_DIM):
    assert col0 % width == 0
    return pl.BlockSpec((1, L, width), lambda b, j: (b, 0, col0 // width + j))


def _bf16_dot(a, b, dims):
    return lax.dot_general(a.astype(jnp.bfloat16), b.astype(jnp.bfloat16), (dims, ((), ())),
                           preferred_element_type=jnp.float32)


_NN = ((1,), (0,))
_NT = ((1,), (1,))
_TN = ((0,), (0,))


def rope_tables(L):
    quarter = HEAD_DIM // 4
    t = jnp.arange(L)
    lane = jnp.arange(HEAD_DIM)
    pos = jnp.where(lane[None, :] < 2 * quarter, (t // GRID_W)[:, None], (t % GRID_W)[:, None]).astype(jnp.float32)
    inv = ROPE_BASE ** (-(lane % quarter).astype(jnp.float32) / quarter)
    ang = pos * inv[None, :]
    sign = jnp.where((lane % (2 * quarter)) < quarter, -1.0, 1.0)
    return jnp.cos(ang), jnp.sin(ang) * sign[None, :]


def _rope(x, cos, sin_signed):
    quarter = HEAD_DIM // 4
    lane = lax.broadcasted_iota(jnp.int32, x.shape, 1)
    partner = jnp.where((lane % (2 * quarter)) < quarter,
                        pltpu.roll(x, HEAD_DIM - quarter, axis=1), pltpu.roll(x, quarter, axis=1))
    return x * cos + partner * sin_signed


def _ret_kernel(*refs, n_chunks, rope):
    if rope:
        (q_ref, k_ref, v_ref, g_ref, sf0_ref, sb0_ref, dec_ref, cos_ref, sin_ref,
         o_ref, sf_ref, sb_ref, kvf, kvb, kr) = refs
    else:
        (q_ref, k_ref, v_ref, g_ref, sf0_ref, sb0_ref, dec_ref,
         o_ref, sf_ref, sb_ref, kvf, kvb, kr) = refs
    C = RET_CHUNK
    lf = dec_ref[0, 0:1, :]
    lb = dec_ref[0, 1:2, :]
    pos = lax.broadcasted_iota(jnp.int32, (C, HEAD_DIM), 0).astype(jnp.float32)
    kdf = jnp.exp(lf * (C - 1.0 - pos))
    kdb = jnp.exp(lb * pos)
    qdf = jnp.exp(lf * (pos + 1.0))
    qdb = jnp.exp(lb * (C - pos))
    cf = jnp.exp(lf * C)
    cb = jnp.exp(lb * C)
    diff = (lax.broadcasted_iota(jnp.int32, (C, C), 0) - lax.broadcasted_iota(jnp.int32, (C, C), 1)).astype(jnp.float32)
    dmat = (jnp.where(diff >= 0, jnp.exp(lf * jnp.maximum(diff, 0.0)), 0.0)
            + jnp.where(diff <= 0, jnp.exp(lb * jnp.maximum(-diff, 0.0)), 0.0))

    def rows(m):
        return pl.ds(pl.multiple_of(m * C, C), C)

    def summaries(m, carry):
        k = k_ref[0, rows(m), :] * RET_SCALE
        if rope:
            k = _rope(k, cos_ref[rows(m), :], sin_ref[rows(m), :])
        kr[rows(m), :] = k
        v = v_ref[0, rows(m), :]
        kvf[m] = _bf16_dot(k * kdf, v, _TN)
        kvb[m] = _bf16_dot(k * kdb, v, _TN)
        return carry

    lax.fori_loop(0, n_chunks, summaries, 0, unroll=4)

    def scan_f(m, s):
        nxt = s * cf + kvf[m]
        kvf[m] = s
        return nxt

    sf_ref[0, 0] = lax.fori_loop(0, n_chunks, scan_f, sf0_ref[0, 0])

    def scan_b(i, s):
        m = n_chunks - 1 - i
        nxt = s * cb + kvb[m]
        kvb[m] = s
        return nxt

    sb_ref[0, 0] = lax.fori_loop(0, n_chunks, scan_b, sb0_ref[0, 0])

    def outputs(m, carry):
        q = q_ref[0, rows(m), :]
        if rope:
            q = _rope(q, cos_ref[rows(m), :], sin_ref[rows(m), :])
        k = kr[rows(m), :]
        v = v_ref[0, rows(m), :]
        a = _bf16_dot(q, k, _NT) * dmat
        o = (_bf16_dot(a, v, _NN) + _bf16_dot(q * qdf, kvf[m], _NN) + _bf16_dot(q * qdb, kvb[m], _NN))
        o = o * lax.rsqrt(jnp.mean(o * o, axis=-1, keepdims=True) + EPS)
        o_ref[0, rows(m), :] = (o * jax.nn.silu(g_ref[0, rows(m), :])).astype(o_ref.dtype)
        return carry

    lax.fori_loop(0, n_chunks, outputs, 0, unroll=4)


def retention(proj, log_f, log_b, s_f, s_b, rope):
    B, L, _ = proj.shape
    n_chunks = L // RET_CHUNK
    dec = jnp.zeros((RET_HEADS, 8, HEAD_DIM), jnp.float32)
    dec = dec.at[:, 0, :].set(log_f[:, None]).at[:, 1, :].set(log_b[:, None])
    state_spec = pl.BlockSpec((1, 1, HEAD_DIM, HEAD_DIM), lambda b, h: (b, h, 0, 0))
    in_specs = [_col_spec(L, KV_COLS + R_RQ), _col_spec(L, K_RK), _col_spec(L, K_RV), _col_spec(L, KV_COLS + R_RG),
                state_spec, state_spec, pl.BlockSpec((1, 8, HEAD_DIM), lambda b, h: (h, 0, 0))]
    args = [proj, proj, proj, proj, s_f, s_b, dec]
    if rope:
        tab_spec = pl.BlockSpec((L, HEAD_DIM), lambda b, h: (0, 0))
        in_specs += [tab_spec, tab_spec]
        args += list(rope_tables(L))
    state_shape = jax.ShapeDtypeStruct((B, RET_HEADS, HEAD_DIM, HEAD_DIM), jnp.float32)
    return pl.pallas_call(
        functools.partial(_ret_kernel, n_chunks=n_chunks, rope=rope),
        grid=(B, RET_HEADS),
        in_specs=in_specs,
        out_specs=[_col_spec(L, 0), state_spec, state_spec],
        out_shape=[jax.ShapeDtypeStruct((B, L, RET_W), jnp.bfloat16), state_shape, state_shape],
        scratch_shapes=[pltpu.VMEM((n_chunks, HEAD_DIM, HEAD_DIM), jnp.float32),
                        pltpu.VMEM((n_chunks, HEAD_DIM, HEAD_DIM), jnp.float32),
                        pltpu.VMEM((L, HEAD_DIM), jnp.float32)],
        compiler_params=pltpu.CompilerParams(
            dimension_semantics=("parallel", "parallel"), vmem_limit_bytes=VMEM_LIMIT_BYTES),
        name="retention",
    )(*args)


def na_bias_windows(rpb):
    col = jnp.arange(GRID_W)
    cstart = jnp.clip(col - NA_KW // 2, 0, GRID_W - NA_KW)
    cmask = (col[None, :] >= cstart[:, None]) & (col[None, :] < cstart[:, None] + NA_KW)
    cidx = jnp.clip(col[None, :] - col[:, None] + NA_KW - 1, 0, 2 * NA_KW - 2)
    rpb_c = jnp.where(cmask[None, None], rpb[:, :, cidx].astype(jnp.float32), NEG_INF)
    masked = jnp.full_like(rpb_c[:, 0], NEG_INF)
    tabs = []
    for w in range(NA_KH):
        for off in range(NA_OFFS):
            tabs.append(jnp.concatenate(
                [rpb_c[:, w + u - off] if 0 <= u - off < NA_KH else masked for u in range(NA_SPAN)], axis=-1))
    return jnp.stack(tabs, axis=1)


NA_ROWS = 2
NA_SPAN = 10
NA_OFFS = NA_SPAN - NA_KH + 1
NA_UNROLL = 4


def _na_kernel(*refs, n_rows, local):
    if local:
        (q_ref, k_ref, v_ref, kc_ref, vc_ref, bw_ref, qn_ref, kn_ref, o_ref, k_s, v_s, kc_s, vc_s) = refs
    else:
        (q_ref, kc_ref, vc_ref, qn_ref, kn_ref, o_ref, kc_s, vc_s) = refs
    scale = HEAD_DIM ** -0.5

    def norm(x, g_ref):
        return x * lax.rsqrt(jnp.mean(x * x, axis=-1, keepdims=True) + EPS) * g_ref[...]

    kc_s[...] = norm(kc_ref[0], kn_ref).astype(jnp.bfloat16)
    vc_s[...] = vc_ref[0].astype(jnp.bfloat16)
    if local:
        k_s[...] = norm(k_ref[0], kn_ref).astype(jnp.bfloat16)
        v_s[...] = v_ref[0].astype(jnp.bfloat16)

    def step(i, carry):
        r0 = i * NA_ROWS
        qrows = pl.ds(pl.multiple_of(r0 * GRID_W, NA_ROWS * GRID_W), NA_ROWS * GRID_W)
        q = norm(q_ref[0, qrows, :], qn_ref).astype(jnp.bfloat16)
        s_ctx = _bf16_dot(q, kc_s[...], _NT) * scale
        m = jnp.max(s_ctx, axis=-1, keepdims=True)
        if local:
            span = jnp.clip(r0 - NA_KH // 2, 0, n_rows - NA_SPAN)
            krows = pl.ds(pl.multiple_of(span * GRID_W, GRID_W), NA_SPAN * GRID_W)
            bias = []
            for j in range(NA_ROWS):
                r = r0 + j
                start = jnp.clip(r - NA_KH // 2, 0, n_rows - NA_KH)
                bias.append(bw_ref[0, (start - r + NA_KH - 1) * NA_OFFS + (start - span)])
            s_loc = _bf16_dot(q, k_s[krows, :], _NT) * scale + jnp.concatenate(bias, axis=0)
            m = jnp.maximum(m, jnp.max(s_loc, axis=-1, keepdims=True))
            p_loc = jnp.exp(s_loc - m)
        p_ctx = jnp.exp(s_ctx - m)
        den = jnp.sum(p_ctx, axis=-1, keepdims=True)
        o = _bf16_dot(p_ctx, vc_s[...], _NN)
        if local:
            den = den + jnp.sum(p_loc, axis=-1, keepdims=True)
            o = o + _bf16_dot(p_loc, v_s[krows, :], _NN)
        o_ref[0, qrows, :] = (o / den).astype(o_ref.dtype)
        return carry

    lax.fori_loop(0, n_rows // NA_ROWS, step, 0, unroll=NA_UNROLL)


def neighbourhood_attention(proj, proj_c, rpb, q_norm, k_norm, local):
    B, L, _ = proj.shape
    Lc = proj_c.shape[1]
    n_rows = L // GRID_W
    norm_spec = pl.BlockSpec((1, HEAD_DIM), lambda b, h: (0, 0))
    qn = q_norm.reshape(1, HEAD_DIM).astype(jnp.float32)
    kn = k_norm.reshape(1, HEAD_DIM).astype(jnp.float32)
    ctx_specs = [_col_spec(Lc, K_NK), _col_spec(Lc, K_NV)]
    ctx_scratch = [pltpu.VMEM((Lc, HEAD_DIM), jnp.bfloat16), pltpu.VMEM((Lc, HEAD_DIM), jnp.bfloat16)]
    assert n_rows % NA_ROWS == 0
    if local:
        assert n_rows >= NA_SPAN
        bw = na_bias_windows(rpb)
        in_specs = ([_col_spec(L, KV_COLS + R_NQ), _col_spec(L, K_NK), _col_spec(L, K_NV)] + ctx_specs
                    + [pl.BlockSpec((1,) + bw.shape[1:], lambda b, h: (h, 0, 0, 0)), norm_spec, norm_spec])
        args = [proj, proj, proj, proj_c, proj_c, bw, qn, kn]
        scratch = [pltpu.VMEM((L, HEAD_DIM), jnp.bfloat16), pltpu.VMEM((L, HEAD_DIM), jnp.bfloat16)] + ctx_scratch
    else:
        in_specs = [_col_spec(L, KV_COLS + R_NQ)] + ctx_specs + [norm_spec, norm_spec]
        args = [proj, proj_c, proj_c, qn, kn]
        scratch = ctx_scratch
    return pl.pallas_call(
        functools.partial(_na_kernel, n_rows=n_rows, local=local),
        grid=(B, NA_HEADS),
        in_specs=in_specs,
        out_specs=_col_spec(L, 0),
        out_shape=jax.ShapeDtypeStruct((B, L, NA_W), jnp.bfloat16),
        scratch_shapes=scratch,
        compiler_params=pltpu.CompilerParams(
            dimension_semantics=("parallel", "parallel"), vmem_limit_bytes=VMEM_LIMIT_BYTES),
        name="neighbourhood_attention",
    )(*args)


def dft_tables(L):
    n = 2 * L
    f = jnp.arange(L, dtype=jnp.int32)[:, None]
    t = jnp.arange(L, dtype=jnp.int32)[None, :]
    ang = ((f * t) % n).astype(jnp.float32) * (2.0 * math.pi / n)
    cm = jnp.cos(ang)
    nyquist = jnp.where(t % 2 == 0, 1.0, -1.0)
    sm = jnp.where(f == 0, nyquist, jnp.sin(ang))
    return cm.astype(jnp.bfloat16), sm.astype(jnp.bfloat16)


def hy_filter_spectrum(k, cm, sm):
    L = k.shape[0]
    kf = k[:, :, 0]
    kb = k[:, :, 1].at[0].set(0.0)
    ksum = (kf + kb).reshape(L, -1)
    kdiff = (kf - kb).reshape(L, -1)
    kc = mm(cm, ksum)
    ks = mm(sm, kdiff)
    sign = jnp.where(jnp.arange(L) % 2 == 0, 1.0, -1.0)[:, None]
    ks = ks.at[0].set(jnp.sum(sign * ksum, axis=0))
    split = lambda a: jnp.moveaxis(a.reshape(L, HY_ORDER, -1), 1, 0)
    return split(kc), split(ks)


def _hy_short_kernel(u_ref, w_ref, b_ref, o_ref):
    u = u_ref[0]
    L = u.shape[0]
    t = lax.broadcasted_iota(jnp.int32, u.shape, 0)
    prev = jnp.where(t == 0, 0.0, pltpu.roll(u, 1, axis=0))
    nxt = jnp.where(t == L - 1, 0.0, pltpu.roll(u, L - 1, axis=0))
    o_ref[0] = prev * w_ref[0:1, :] + u * w_ref[1:2, :] + nxt * w_ref[2:3, :] + b_ref[...]


HY_SHORT_COLS = 256


def hy_short(proj, conv_w, conv_b):
    B, L, _ = proj.shape
    cols = HY_SHORT_COLS
    n = 3 * HY_W
    return pl.pallas_call(
        _hy_short_kernel,
        grid=(B, n // cols),
        in_specs=[_col_spec(L, KV_COLS + R_HY, cols),
                  pl.BlockSpec((HY_SHORT, cols), lambda b, j: (0, j)),
                  pl.BlockSpec((1, cols), lambda b, j: (0, j))],
        out_specs=_col_spec(L, 0, cols),
        out_shape=jax.ShapeDtypeStruct((B, L, n), jnp.float32),
        compiler_params=pltpu.CompilerParams(
            dimension_semantics=("parallel", "parallel"), vmem_limit_bytes=VMEM_LIMIT_BYTES),
        name="hy_short",
    )(proj, conv_w.astype(jnp.float32), conv_b.reshape(1, n).astype(jnp.float32))


def _hy_fwd_kernel(cm_ref, sm_ref, u_ref, kc_ref, ks_ref, yc_ref, ys_ref):
    u = u_ref[0]
    uc = jnp.dot(cm_ref[...], u, preferred_element_type=jnp.float32)
    us = jnp.dot(sm_ref[...], u, preferred_element_type=jnp.float32)
    kc = kc_ref[...]
    ks = ks_ref[...]
    fb = uc.shape[0]
    f = pl.program_id(0) * fb + lax.broadcasted_iota(jnp.int32, (fb, 1), 0)
    packed = f == 0
    wgt = jnp.where(packed, 1.0, 2.0) / (2 * cm_ref.shape[1])
    yc_ref[0] = (jnp.where(packed, uc * kc, uc * kc - us * ks) * wgt).astype(yc_ref.dtype)
    ys_ref[0] = (jnp.where(packed, us * ks, uc * ks + us * kc) * wgt).astype(ys_ref.dtype)


HY_DFT_ROWS = 512


def hy_fwd(u, cm, sm, kc, ks):
    B, L, C = u.shape
    fb = min(HY_DFT_ROWS, L)
    mat_spec = pl.BlockSpec((fb, L), lambda f, b: (f, 0))
    k_spec = pl.BlockSpec((fb, C), lambda f, b: (f, 0))
    y_spec = pl.BlockSpec((1, fb, C), lambda f, b: (b, f, 0))
    y_shape = jax.ShapeDtypeStruct((B, L, C), jnp.bfloat16)
    return pl.pallas_call(
        _hy_fwd_kernel,
        grid=(L // fb, B),
        in_specs=[mat_spec, mat_spec, pl.BlockSpec((1, L, C), lambda f, b: (b, 0, 0)), k_spec, k_spec],
        out_specs=[y_spec, y_spec],
        out_shape=[y_shape, y_shape],
        compiler_params=pltpu.CompilerParams(
            dimension_semantics=("parallel", "parallel"), vmem_limit_bytes=VMEM_LIMIT_BYTES),
        name="hy_fwd",
    )(cm, sm, u, kc, ks)


def _hy_inv_kernel(cm_ref, sm_ref, yc_ref, ys_ref, u_ref, gate_ref, skip_ref, o_ref):
    conv = (lax.dot_general(cm_ref[...], yc_ref[0], (_TN, ((), ())), preferred_element_type=jnp.float32)
            + lax.dot_general(sm_ref[...], ys_ref[0], (_TN, ((), ())), preferred_element_type=jnp.float32))
    o_ref[0] = (gate_ref[0] * (conv + u_ref[0] * skip_ref[...])).astype(o_ref.dtype)


HY_INV_COLS = 512


def hy_inv(yc, ys, cm, sm, u, u_col0, gate, gate_col0, skip, out_dtype):
    B, L, C = yc.shape
    tb = min(HY_DFT_ROWS, L)
    cols = HY_INV_COLS
    once = pl.Buffered(1)
    mat_spec = pl.BlockSpec((L, tb), lambda b, j, t: (0, t))
    y_spec = pl.BlockSpec((1, L, cols), lambda b, j, t: (b, 0, j), pipeline_mode=once)

    def win(col0):
        return pl.BlockSpec((1, tb, cols), lambda b, j, t: (b, t, col0 // cols + j))

    return pl.pallas_call(
        _hy_inv_kernel,
        grid=(B, C // cols, L // tb),
        in_specs=[mat_spec, mat_spec, y_spec, y_spec, win(u_col0), win(gate_col0),
                  pl.BlockSpec((1, cols), lambda b, j, t: (0, j))],
        out_specs=win(0),
        out_shape=jax.ShapeDtypeStruct((B, L, C), out_dtype),
        compiler_params=pltpu.CompilerParams(
            dimension_semantics=("parallel", "parallel", "arbitrary"), vmem_limit_bytes=VMEM_LIMIT_BYTES),
        name="hy_inv",
    )(cm, sm, yc, ys, u, gate, skip.reshape(1, C).astype(jnp.float32))


def hyena(proj, conv_w, conv_b, filt, skip, tables):
    B, L, _ = proj.shape
    cm, sm = tables
    hyu = hy_short(proj, conv_w, conv_b)
    kc, ks = hy_filter_spectrum(hyena_filters(L, *filt), cm, sm)
    v = hyu[..., :HY_W].astype(jnp.bfloat16)
    yc, ys = hy_fwd(v, cm, sm, kc[0], ks[0])
    z1 = hy_inv(yc, ys, cm, sm, hyu, 0, hyu, HY_W, skip[0], jnp.float32)
    yc, ys = hy_fwd(z1.astype(jnp.bfloat16), cm, sm, kc[1], ks[1])
    return hy_inv(yc, ys, cm, sm, z1, 0, hyu, 2 * HY_W, skip[1], jnp.bfloat16)


def mixer(proj, proj_c, p, log_f, log_b, s_f, s_b, tables, latent):
    hy = hyena(proj, p['hy_conv_w'], p['hy_conv_b'], p['hy_f'], p['hy_skip'], tables)
    ro, sf, sb = retention(proj, log_f, log_b, s_f, s_b, rope=latent)
    no = neighbourhood_attention(proj, proj_c, p['na_rpb'], p['q_norm'], p['k_norm'], local=latent)
    return mm(jnp.concatenate([hy, ro, no], axis=-1), p['w_out']), sf, sb


def _norm_mod_kernel(*refs, residual):
    if residual:
        x_ref, g_ref, shift_ref, scale_ref, res_ref, gate_ref, xo_ref, h_ref = refs
        x = x_ref[0] + gate_ref[0] * res_ref[0]
        xo_ref[0] = x
    else:
        x_ref, g_ref, shift_ref, scale_ref, h_ref = refs
        x = x_ref[0]
    y = x * lax.rsqrt(jnp.mean(x * x, axis=-1, keepdims=True) + EPS) * g_ref[...]
    h_ref[0] = (y * (1.0 + scale_ref[0]) + shift_ref[0]).astype(h_ref.dtype)


NORM_ROWS = 256


def norm_mod(x, g, shift, scale, res=None, gate=None):
    B, L, D = x.shape
    rb = min(NORM_ROWS, L)
    residual = res is not None

    def vec_spec(v):
        per_batch = v.shape[0] > 1
        return pl.BlockSpec((1, 1, D), lambda b, r: (b if per_batch else 0, 0, 0))

    row_spec = pl.BlockSpec((1, rb, D), lambda b, r: (b, r, 0))
    in_specs = [row_spec, pl.BlockSpec((1, D), lambda b, r: (0, 0)), vec_spec(shift), vec_spec(scale)]
    args = [x, g.reshape(1, D).astype(jnp.float32), shift, scale]
    h_shape = jax.ShapeDtypeStruct((B, L, D), jnp.bfloat16)
    if residual:
        in_specs += [row_spec, vec_spec(gate)]
        args += [res, gate]
        out_specs, out_shape = [row_spec, row_spec], [jax.ShapeDtypeStruct((B, L, D), jnp.float32), h_shape]
    else:
        out_specs, out_shape = row_spec, h_shape
    return pl.pallas_call(
        functools.partial(_norm_mod_kernel, residual=residual),
        grid=(B, L // rb),
        in_specs=in_specs,
        out_specs=out_specs,
        out_shape=out_shape,
        compiler_params=pltpu.CompilerParams(
            dimension_semantics=("parallel", "parallel"), vmem_limit_bytes=VMEM_LIMIT_BYTES),
        name="norm_mod",
    )(*args)


def layer(xl, xc, mod_l, mod_c, p, last, tables_l, tables_c):
    sa_l, ca_l, ga_l, sf_l, cf_l, gf_l = jnp.split(mod_l, N_MOD, axis=-1)
    mod_c_parts = jnp.split(mod_c, mod_c.shape[-1] // D_MODEL, axis=-1)
    sa_c, ca_c = mod_c_parts[0], mod_c_parts[1]
    log_f = -jnp.exp(p['ret_decay'][0].astype(jnp.float32))
    log_b = -jnp.exp(p['ret_decay'][1].astype(jnp.float32))
    w_in = p['w_in'].astype(jnp.bfloat16)
    hc = norm_mod(xc, p['norm_mix'], sa_c, ca_c)
    proj_c = mm(hc, w_in)
    zeros = jnp.zeros((xc.shape[0], RET_HEADS, HEAD_DIM, HEAD_DIM), jnp.float32)
    if last:
        _, s_f, s_b = retention(proj_c, log_f, log_b, zeros, zeros, rope=False)
    else:
        y_c, s_f, s_b = mixer(proj_c, proj_c, p, log_f, log_b, zeros, zeros, tables_c, False)
    hl = norm_mod(xl, p['norm_mix'], sa_l, ca_l)
    y_l, _, _ = mixer(mm(hl, w_in), proj_c, p, log_f, log_b, s_f, s_b, tables_l, True)
    xl, hf_l = norm_mod(xl, p['norm_ffn'], sf_l, cf_l, y_l, ga_l)
    xl = xl + gf_l * peer(hf_l, p['peer_wq'], p['peer_keys'], p['peer_u'], p['peer_v'])
    if last:
        return xl, None
    ga_c, sf_c, cf_c, gf_c = mod_c_parts[2], mod_c_parts[3], mod_c_parts[4], mod_c_parts[5]
    xc, hf_c = norm_mod(xc, p['norm_ffn'], sf_c, cf_c, y_c, ga_c)
    xc = xc + gf_c * peer(hf_c, p['peer_wq'], p['peer_keys'], p['peer_u'], p['peer_v'])
    return xl, xc


def kernel(x, c, ctx, c_ctx, w_mod, b_mod, norm_mix, w_in, hy_conv_w, hy_conv_b,
           hy_f_w1, hy_f_b1, hy_f_freq, hy_f_w2, hy_f_b2, hy_f_w3, hy_f_b3, hy_f_w4,
           hy_skip, ret_decay, na_rpb, q_norm, k_norm, w_out, norm_ffn,
           peer_wq, peer_keys, peer_u, peer_v):
    xl, xc = x, ctx
    sc = jax.nn.silu(c)
    scc = jax.nn.silu(c_ctx)
    tables_l = dft_tables(x.shape[1])
    tables_c = dft_tables(ctx.shape[1])
    for l in range(DEPTH):
        last = l == DEPTH - 1
        p = {
            'norm_mix': norm_mix[l], 'w_in': w_in[l],
            'hy_conv_w': hy_conv_w[l], 'hy_conv_b': hy_conv_b[l],
            'hy_f': (hy_f_w1[l], hy_f_b1[l], hy_f_freq[l], hy_f_w2[l], hy_f_b2[l],
                     hy_f_w3[l], hy_f_b3[l], hy_f_w4[l]),
            'hy_skip': hy_skip[l], 'ret_decay': ret_decay[l], 'na_rpb': na_rpb[l],
            'q_norm': q_norm[l], 'k_norm': k_norm[l], 'w_out': w_out[l],
            'norm_ffn': norm_ffn[l], 'peer_wq': peer_wq[l], 'peer_keys': peer_keys[l],
            'peer_u': peer_u[l].astype(jnp.bfloat16), 'peer_v': peer_v[l].astype(jnp.bfloat16),
        }
        mod_l = (sc @ w_mod[l] + b_mod[l])[:, None, :]
        n_ctx = 2 if last else N_MOD
        mod_c = (scc @ w_mod[l][:, :n_ctx * D_MODEL] + b_mod[l][:n_ctx * D_MODEL])[None, None, :]
        xl, xc = layer(xl, xc, mod_l, mod_c, p, last, tables_l, tables_c)
    return xl
```

```python
import functools
import math

import jax
import jax.numpy as jnp
from jax import lax
from jax.experimental import pallas as pl
from jax.experimental.pallas import tpu as pltpu

D_MODEL = 4096
BATCH = 4
SEQ = 4096
DEPTH = 2

GRID_W = 64
CTX_LEN = 256
HEAD_DIM = 128
N_MIX_HEADS = D_MODEL // HEAD_DIM
HY_GROUPS = N_MIX_HEADS // 4
RET_HEADS = (N_MIX_HEADS - HY_GROUPS) // 2
NA_HEADS = N_MIX_HEADS - HY_GROUPS - RET_HEADS
HY_W = HY_GROUPS * HEAD_DIM
RET_W = RET_HEADS * HEAD_DIM
NA_W = NA_HEADS * HEAD_DIM
K_RK = 0
K_RV = K_RK + RET_W
K_NK = K_RV + RET_W
K_NV = K_NK + NA_W
KV_COLS = K_NV + NA_W
R_HY = 0
R_RQ = R_HY + 3 * HY_W
R_RG = R_RQ + RET_W
R_NQ = R_RG + RET_W
REST_COLS = R_NQ + NA_W
IN_COLS = KV_COLS + REST_COLS
N_MOD = 6
HY_ORDER = 2
HY_SHORT = 3
HY_EMB = 33
HY_BANDS = (HY_EMB - 1) // 2
HY_FILTER_HIDDEN = 64
HY_DECAY_TARGET = 1e-2
HY_SHORT_DECAY_PCT = 0.3
HY_LONG_DECAY_PCT = 1.5
HY_MAX_DECAY = math.log(HY_DECAY_TARGET) / HY_SHORT_DECAY_PCT
HY_MIN_DECAY = math.log(HY_DECAY_TARGET) / HY_LONG_DECAY_PCT
HY_FILTER_SCALE = 0.05
RET_CHUNK = 128
RET_SCALE = HEAD_DIM ** -0.5
ROPE_BASE = 10000.0
NA_KH = 8
NA_KW = 16
PEER_HEADS = 8
PEER_NKEYS = 128
PEER_N = PEER_NKEYS * PEER_NKEYS
PEER_TOPK = 16
PEER_DK = 256
PEER_BLOCK = 64
EPS = 1e-6
NEG_INF = -1e30

VMEM_LIMIT_BYTES = 56 * 1024 * 1024
BF16_ROWS = 16


def _mm_kernel(a_ref, b_ref, o_ref):
    o_ref[...] = jnp.dot(a_ref[...], b_ref[...], preferred_element_type=jnp.float32)


def _pick_block(n, candidates):
    for c in candidates:
        if n % c == 0:
            return c
    return n


def mm(a, b):
    lead = a.shape[:-1]
    K = a.shape[-1]
    a2 = a.reshape(-1, K).astype(jnp.bfloat16)
    b2 = b.astype(jnp.bfloat16)
    M, N = a2.shape[0], b2.shape[1]
    bm = _pick_block(M, (1024, 512, 256, 128, 64, 32, 16, 8))
    bn = _pick_block(N, (1024, 768, 512, 256, 128))
    out = pl.pallas_call(
        _mm_kernel,
        grid=(M // bm, N // bn),
        in_specs=[pl.BlockSpec((bm, K), lambda i, j: (i, 0)),
                  pl.BlockSpec((K, bn), lambda i, j: (0, j))],
        out_specs=pl.BlockSpec((bm, bn), lambda i, j: (i, j)),
        out_shape=jax.ShapeDtypeStruct((M, N), jnp.float32),
        compiler_params=pltpu.CompilerParams(
            dimension_semantics=("parallel", "parallel"),
            vmem_limit_bytes=VMEM_LIMIT_BYTES),
        name="mm",
    )(a2, b2)
    return out.reshape(lead + (N,))


def hyena_filters(L, w1, b1, freq, w2, b2, w3, b3, w4):
    f32 = lambda a: a.astype(jnp.float32)
    t = jnp.linspace(0.0, 1.0, L, dtype=jnp.float32)[:, None]
    w = 2.0 * math.pi * jnp.arange(L, dtype=jnp.float32)[:, None] / L
    f = jnp.linspace(1e-4, HY_BANDS - 1, HY_BANDS, dtype=jnp.float32)[None, :]
    z = jnp.concatenate([t, jnp.cos(f * w), -jnp.sin(f * w)], axis=-1)
    h = jnp.sin(f32(freq) * (z @ f32(w1) + f32(b1)))
    h = jnp.sin(f32(freq) * (h @ f32(w2) + f32(b2)))
    h = jnp.sin(f32(freq) * (h @ f32(w3) + f32(b3)))
    k = (h @ f32(w4)).reshape(L, HY_ORDER, 2, HY_W)
    deltas = jnp.abs(jnp.linspace(HY_MIN_DECAY, HY_MAX_DECAY, HY_W, dtype=jnp.float32))
    return k * jnp.exp(-t[:, :, None, None] * deltas)


def _mm_nt_kernel(w_ref, x_ref, o_ref):
    o_ref[...] = lax.dot_general(w_ref[...], x_ref[...], (((1,), (1,)), ((), ())),
                                 preferred_element_type=jnp.float32)


def mm_nt(w, x):
    N, K = w.shape
    M = x.shape[0]
    bn = _pick_block(N, (1024, 512, 256, 128))
    bm = _pick_block(M, (1024, 512, 256, 128))
    return pl.pallas_call(
        _mm_nt_kernel,
        grid=(M // bm, N // bn),
        in_specs=[pl.BlockSpec((bn, K), lambda i, j: (j, 0)),
                  pl.BlockSpec((bm, K), lambda i, j: (i, 0))],
        out_specs=pl.BlockSpec((bn, bm), lambda i, j: (j, i)),
        out_shape=jax.ShapeDtypeStruct((N, M), jnp.float32),
        compiler_params=pltpu.CompilerParams(
            dimension_semantics=("parallel", "parallel"),
            vmem_limit_bytes=VMEM_LIMIT_BYTES),
        name="mm_nt",
    )(w.astype(jnp.bfloat16), x.astype(jnp.bfloat16))


def _top_values(s, k, want_rank=False):
    vals = []
    work = s
    rank = jnp.full(s.shape, float(k), jnp.float32)
    for r in range(k):
        m = jnp.max(work, axis=0, keepdims=True)
        vals.append(m)
        hit = work == m
        if want_rank:
            rank = jnp.where(hit, float(r), rank)
        work = jnp.where(hit, -jnp.inf, work)
    vals = jnp.concatenate(vals, axis=0)
    return (vals, rank) if want_rank else vals


def _peer_route_kernel(qt_ref, keys_ref, r1_ref, b_ref, n_ref, a_ref):
    half = PEER_DK // 2
    K = PEER_TOPK
    for h in range(PEER_HEADS):
        s = []
        for p in range(2):
            q = qt_ref[pl.ds((2 * h + p) * half, half), :]
            s.append(jnp.dot(keys_ref[2 * h + p], q, preferred_element_type=jnp.float32,
                             precision=lax.Precision.HIGHEST))
        s0, s1 = s
        v0 = _top_values(s0, K)
        v1, rank1 = _top_values(s1, K, want_rank=True)
        v0_tail = jnp.where(lax.broadcasted_iota(jnp.int32, v0.shape, 0) < 4, -jnp.inf, v0)
        cand = jnp.concatenate([v0[0:1] + v1] + [v0[a:a + 1] + v1[0:8] for a in (1, 2, 3)]
                               + [v1[b:b + 1] + v0_tail for b in (0, 1, 2)], axis=0)
        tau = _top_values(cand, K)[K - 1:K]
        m = v0[0:1] + v1[0:1]
        z = jnp.sum(jnp.where(cand >= tau, jnp.exp(cand - m), 0.0), axis=0, keepdims=True)
        n = jnp.zeros_like(s0)
        for b in range(K):
            n = n + jnp.where(s0 + v1[b:b + 1] >= tau, 1.0, 0.0)
        r1_ref[h] = rank1.astype(r1_ref.dtype)
        b_ref[h] = jnp.exp(s1 - v1[0:1]).astype(b_ref.dtype)
        n_ref[h] = n
        a_ref[h] = jnp.exp(s0 - v0[0:1]) / z


PEER_ROUTE_LANES = 128


def peer_route(qt, sub_keys):
    T = qt.shape[1]
    keys = sub_keys.reshape(PEER_HEADS * 2, PEER_NKEYS, PEER_DK // 2).astype(jnp.float32)
    bt = PEER_ROUTE_LANES
    shape = (PEER_HEADS, PEER_NKEYS, T)
    ospec = pl.BlockSpec((PEER_HEADS, PEER_NKEYS, bt), lambda i: (0, 0, i))
    return pl.pallas_call(
        _peer_route_kernel,
        grid=(T // bt,),
        in_specs=[pl.BlockSpec((PEER_HEADS * PEER_DK, bt), lambda i: (0, i)),
                  pl.BlockSpec(keys.shape, lambda i: (0, 0, 0))],
        out_specs=[ospec] * 4,
        out_shape=[jax.ShapeDtypeStruct(shape, jnp.bfloat16), jax.ShapeDtypeStruct(shape, jnp.bfloat16),
                   jax.ShapeDtypeStruct(shape, jnp.float32), jax.ShapeDtypeStruct(shape, jnp.float32)],
        compiler_params=pltpu.CompilerParams(
            dimension_semantics=("parallel",), vmem_limit_bytes=VMEM_LIMIT_BYTES),
        name="peer_route",
    )(qt, keys)


PEER_DENSE_TOKENS = 512
PEER_DENSE_GROUPS = 4
PEER_LANE_CHUNK = 128
PEER_SPLIT_V = False
PEER_U_PARTS = 1


def _peer_dense_kernel(x_ref, u_ref, v_ref, r1_ref, b_ref, n_ref, a_ref, o_ref, n_s, a_s, pre_s, act_s, *, groups):
    e = pl.program_id(1)
    bt = x_ref.shape[0]

    @pl.when(e == 0)
    def _():
        o_ref[...] = jnp.zeros_like(o_ref)

    for g in range(groups):
        for h in range(PEER_HEADS):
            i = e * groups + g
            n_s[g * PEER_HEADS + h] = jnp.broadcast_to(n_ref[h, pl.ds(i, 1), :], (BF16_ROWS, bt)).astype(jnp.bfloat16)
            a_s[g * PEER_HEADS + h] = jnp.broadcast_to(a_ref[h, pl.ds(i, 1), :], (BF16_ROWS, bt)).astype(jnp.bfloat16)

    half_groups = groups // 2
    half_rows = half_groups * PEER_NKEYS
    for part in range(PEER_U_PARTS):
        rows = pl.ds(part * (2 * half_rows // PEER_U_PARTS), 2 * half_rows // PEER_U_PARTS)
        pre_s[rows, :] = lax.dot_general(u_ref[rows, :], x_ref[...], (((1,), (1,)), ((), ())),
                                         preferred_element_type=jnp.float32)
    for half in range(2):
        for g in range(half_groups):
            gi = half * half_groups + g
            rows_g = pl.ds(gi * PEER_NKEYS, PEER_NKEYS)
            for c in range(x_ref.shape[0] // PEER_LANE_CHUNK):
                lanes = pl.ds(c * PEER_LANE_CHUNK, PEER_LANE_CHUNK)
                w = jnp.zeros((PEER_NKEYS, PEER_LANE_CHUNK), jnp.bfloat16)
                for h in range(PEER_HEADS):
                    r = gi * PEER_HEADS + h
                    n_i = jnp.concatenate([n_s[r, :, lanes]] * (PEER_NKEYS // BF16_ROWS), axis=0)
                    a_i = jnp.concatenate([a_s[r, :, lanes]] * (PEER_NKEYS // BF16_ROWS), axis=0)
                    w = w + jnp.where(r1_ref[h, :, lanes] < n_i, b_ref[h, :, lanes], jnp.zeros((), jnp.bfloat16)) * a_i
                act_s[rows_g, lanes] = jax.nn.gelu(pre_s[rows_g, lanes]).astype(jnp.bfloat16) * w
        if PEER_SPLIT_V:
            rows = pl.ds(half * half_rows, half_rows)
            o_ref[...] += lax.dot_general(act_s[rows, :], v_ref[rows, :], (((0,), (0,)), ((), ())),
                                          preferred_element_type=jnp.float32)
    if not PEER_SPLIT_V:
        o_ref[...] += lax.dot_general(act_s[...], v_ref[...], (((0,), (0,)), ((), ())),
                                      preferred_element_type=jnp.float32)


def peer_dense(x, u, v, r1, b, n, a):
    T, D = x.shape
    bt = min(PEER_DENSE_TOKENS, T)
    groups = PEER_DENSE_GROUPS
    be = groups * PEER_NKEYS
    once = pl.Buffered(1)
    rspec = pl.BlockSpec((PEER_HEADS, PEER_NKEYS, bt), lambda t, e: (0, 0, t), pipeline_mode=once)
    return pl.pallas_call(
        functools.partial(_peer_dense_kernel, groups=groups),
        grid=(T // bt, PEER_N // be),
        in_specs=[pl.BlockSpec((bt, D), lambda t, e: (t, 0), pipeline_mode=once),
                  pl.BlockSpec((be, D), lambda t, e: (e, 0)),
                  pl.BlockSpec((be, D), lambda t, e: (e, 0)),
                  rspec, rspec, rspec, rspec],
        out_specs=pl.BlockSpec((bt, D), lambda t, e: (t, 0)),
        out_shape=jax.ShapeDtypeStruct((T, D), jnp.float32),
        scratch_shapes=[pltpu.VMEM((groups * PEER_HEADS, BF16_ROWS, bt), jnp.bfloat16),
                        pltpu.VMEM((groups * PEER_HEADS, BF16_ROWS, bt), jnp.bfloat16),
                        pltpu.VMEM((be, bt), jnp.float32),
                        pltpu.VMEM((be, bt), jnp.bfloat16)],
        compiler_params=pltpu.CompilerParams(
            dimension_semantics=("parallel", "arbitrary"),
            vmem_limit_bytes=VMEM_LIMIT_BYTES),
        name="peer_dense",
    )(x, u, v, r1, b, n, a)


def peer(h, w_q, sub_keys, exp_u, exp_v):
    B, L, D = h.shape
    x = h.reshape(B * L, D).astype(jnp.bfloat16)
    qt = mm_nt(w_q.T, x)
    r1, b, n, a = peer_route(qt, sub_keys)
    return peer_dense(x, exp_u, exp_v, r1, b, n, a).reshape(B, L, D)


def _col_spec(L, col0, width=HEAD_DIM):
    assert col0 % width == 0
    return pl.BlockSpec((1, L, width), lambda b, j: (b, 0, col0 // width + j))


def _bf16_dot(a, b, dims):
    return lax.dot_general(a.astype(jnp.bfloat16), b.astype(jnp.bfloat16), (dims, ((), ())),
                           preferred_element_type=jnp.float32)


_NN = ((1,), (0,))
_NT = ((1,), (1,))
_TN = ((0,), (0,))


def rope_tables(L):
    quarter = HEAD_DIM // 4
    t = jnp.arange(L)
    lane = jnp.arange(HEAD_DIM)
    pos = jnp.where(lane[None, :] < 2 * quarter, (t // GRID_W)[:, None], (t % GRID_W)[:, None]).astype(jnp.float32)
    inv = ROPE_BASE ** (-(lane % quarter).astype(jnp.float32) / quarter)
    ang = pos * inv[None, :]
    sign = jnp.where((lane % (2 * quarter)) < quarter, -1.0, 1.0)
    return jnp.cos(ang), jnp.sin(ang) * sign[None, :]


def _rope(x, cos, sin_signed):
    quarter = HEAD_DIM // 4
    lane = lax.broadcasted_iota(jnp.int32, x.shape, 1)
    partner = jnp.where((lane % (2 * quarter)) < quarter,
                        pltpu.roll(x, HEAD_DIM - quarter, axis=1), pltpu.roll(x, quarter, axis=1))
    return x * cos + partner * sin_signed


def _ret_kernel(*refs, n_chunks, rope):
    if rope:
        (q_ref, k_ref, v_ref, g_ref, sf0_ref, sb0_ref, dec_ref, cos_ref, sin_ref,
         o_ref, sf_ref, sb_ref, kvf, kvb, kr) = refs
    else:
        (q_ref, k_ref, v_ref, g_ref, sf0_ref, sb0_ref, dec_ref,
         o_ref, sf_ref, sb_ref, kvf, kvb, kr) = refs
    C = RET_CHUNK
    lf = dec_ref[0, 0:1, :]
    lb = dec_ref[0, 1:2, :]
    pos = lax.broadcasted_iota(jnp.int32, (C, HEAD_DIM), 0).astype(jnp.float32)
    kdf = jnp.exp(lf * (C - 1.0 - pos))
    kdb = jnp.exp(lb * pos)
    qdf = jnp.exp(lf * (pos + 1.0))
    qdb = jnp.exp(lb * (C - pos))
    cf = jnp.exp(lf * C)
    cb = jnp.exp(lb * C)
    diff = (lax.broadcasted_iota(jnp.int32, (C, C), 0) - lax.broadcasted_iota(jnp.int32, (C, C), 1)).astype(jnp.float32)
    dmat = (jnp.where(diff >= 0, jnp.exp(lf * jnp.maximum(diff, 0.0)), 0.0)
            + jnp.where(diff <= 0, jnp.exp(lb * jnp.maximum(-diff, 0.0)), 0.0))

    def rows(m):
        return pl.ds(pl.multiple_of(m * C, C), C)

    def summaries(m, carry):
        k = k_ref[0, rows(m), :] * RET_SCALE
        if rope:
            k = _rope(k, cos_ref[rows(m), :], sin_ref[rows(m), :])
        kr[rows(m), :] = k
        v = v_ref[0, rows(m), :]
        kvf[m] = _bf16_dot(k * kdf, v, _TN)
        kvb[m] = _bf16_dot(k * kdb, v, _TN)
        return carry

    lax.fori_loop(0, n_chunks, summaries, 0, unroll=4)

    def scan_f(m, s):
        nxt = s * cf + kvf[m]
        kvf[m] = s
        return nxt

    sf_ref[0, 0] = lax.fori_loop(0, n_chunks, scan_f, sf0_ref[0, 0])

    def scan_b(i, s):
        m = n_chunks - 1 - i
        nxt = s * cb + kvb[m]
        kvb[m] = s
        return nxt

    sb_ref[0, 0] = lax.fori_loop(0, n_chunks, scan_b, sb0_ref[0, 0])

    def outputs(m, carry):
        q = q_ref[0, rows(m), :]
        if rope:
            q = _rope(q, cos_ref[rows(m), :], sin_ref[rows(m), :])
        k = kr[rows(m), :]
        v = v_ref[0, rows(m), :]
        a = _bf16_dot(q, k, _NT) * dmat
        o = (_bf16_dot(a, v, _NN) + _bf16_dot(q * qdf, kvf[m], _NN) + _bf16_dot(q * qdb, kvb[m], _NN))
        o = o * lax.rsqrt(jnp.mean(o * o, axis=-1, keepdims=True) + EPS)
        o_ref[0, rows(m), :] = (o * jax.nn.silu(g_ref[0, rows(m), :])).astype(o_ref.dtype)
        return carry

    lax.fori_loop(0, n_chunks, outputs, 0, unroll=4)


def retention(proj, log_f, log_b, s_f, s_b, rope):
    B, L, _ = proj.shape
    n_chunks = L // RET_CHUNK
    dec = jnp.zeros((RET_HEADS, 8, HEAD_DIM), jnp.float32)
    dec = dec.at[:, 0, :].set(log_f[:, None]).at[:, 1, :].set(log_b[:, None])
    state_spec = pl.BlockSpec((1, 1, HEAD_DIM, HEAD_DIM), lambda b, h: (b, h, 0, 0))
    in_specs = [_col_spec(L, KV_COLS + R_RQ), _col_spec(L, K_RK), _col_spec(L, K_RV), _col_spec(L, KV_COLS + R_RG),
                state_spec, state_spec, pl.BlockSpec((1, 8, HEAD_DIM), lambda b, h: (h, 0, 0))]
    args = [proj, proj, proj, proj, s_f, s_b, dec]
    if rope:
        tab_spec = pl.BlockSpec((L, HEAD_DIM), lambda b, h: (0, 0))
        in_specs += [tab_spec, tab_spec]
        args += list(rope_tables(L))
    state_shape = jax.ShapeDtypeStruct((B, RET_HEADS, HEAD_DIM, HEAD_DIM), jnp.float32)
    return pl.pallas_call(
        functools.partial(_ret_kernel, n_chunks=n_chunks, rope=rope),
        grid=(B, RET_HEADS),
        in_specs=in_specs,
        out_specs=[_col_spec(L, 0), state_spec, state_spec],
        out_shape=[jax.ShapeDtypeStruct((B, L, RET_W), jnp.bfloat16), state_shape, state_shape],
        scratch_shapes=[pltpu.VMEM((n_chunks, HEAD_DIM, HEAD_DIM), jnp.float32),
                        pltpu.VMEM((n_chunks, HEAD_DIM, HEAD_DIM), jnp.float32),
                        pltpu.VMEM((L, HEAD_DIM), jnp.float32)],
        compiler_params=pltpu.CompilerParams(
            dimension_semantics=("parallel", "parallel"), vmem_limit_bytes=VMEM_LIMIT_BYTES),
        name="retention",
    )(*args)


def na_bias_windows(rpb):
    col = jnp.arange(GRID_W)
    cstart = jnp.clip(col - NA_KW // 2, 0, GRID_W - NA_KW)
    cmask = (col[None, :] >= cstart[:, None]) & (col[None, :] < cstart[:, None] + NA_KW)
    cidx = jnp.clip(col[None, :] - col[:, None] + NA_KW - 1, 0, 2 * NA_KW - 2)
    rpb_c = jnp.where(cmask[None, None], rpb[:, :, cidx].astype(jnp.float32), NEG_INF)
    masked = jnp.full_like(rpb_c[:, 0], NEG_INF)
    tabs = []
    for w in range(NA_KH):
        for off in range(NA_OFFS):
            tabs.append(jnp.concatenate(
                [rpb_c[:, w + u - off] if 0 <= u - off < NA_KH else masked for u in range(NA_SPAN)], axis=-1))
    return jnp.stack(tabs, axis=1)


NA_ROWS = 2
NA_SPAN = 10
NA_OFFS = NA_SPAN - NA_KH + 1
NA_UNROLL = 4


def _na_kernel(*refs, n_rows, local):
    if local:
        (q_ref, k_ref, v_ref, kc_ref, vc_ref, bw_ref, qn_ref, kn_ref, o_ref, k_s, v_s, kc_s, vc_s) = refs
    else:
        (q_ref, kc_ref, vc_ref, qn_ref, kn_ref, o_ref, kc_s, vc_s) = refs
    scale = HEAD_DIM ** -0.5

    def norm(x, g_ref):
        return x * lax.rsqrt(jnp.mean(x * x, axis=-1, keepdims=True) + EPS) * g_ref[...]

    kc_s[...] = norm(kc_ref[0], kn_ref).astype(jnp.bfloat16)
    vc_s[...] = vc_ref[0].astype(jnp.bfloat16)
    if local:
        k_s[...] = norm(k_ref[0], kn_ref).astype(jnp.bfloat16)
        v_s[...] = v_ref[0].astype(jnp.bfloat16)

    def step(i, carry):
        r0 = i * NA_ROWS
        qrows = pl.ds(pl.multiple_of(r0 * GRID_W, NA_ROWS * GRID_W), NA_ROWS * GRID_W)
        q = norm(q_ref[0, qrows, :], qn_ref).astype(jnp.bfloat16)
        s_ctx = _bf16_dot(q, kc_s[...], _NT) * scale
        m = jnp.max(s_ctx, axis=-1, keepdims=True)
        if local:
            span = jnp.clip(r0 - NA_KH // 2, 0, n_rows - NA_SPAN)
            krows = pl.ds(pl.multiple_of(span * GRID_W, GRID_W), NA_SPAN * GRID_W)
            bias = []
            for j in range(NA_ROWS):
                r = r0 + j
                start = jnp.clip(r - NA_KH // 2, 0, n_rows - NA_KH)
                bias.append(bw_ref[0, (start - r + NA_KH - 1) * NA_OFFS + (start - span)])
            s_loc = _bf16_dot(q, k_s[krows, :], _NT) * scale + jnp.concatenate(bias, axis=0)
            m = jnp.maximum(m, jnp.max(s_loc, axis=-1, keepdims=True))
            p_loc = jnp.exp(s_loc - m)
        p_ctx = jnp.exp(s_ctx - m)
        den = jnp.sum(p_ctx, axis=-1, keepdims=True)
        o = _bf16_dot(p_ctx, vc_s[...], _NN)
        if local:
            den = den + jnp.sum(p_loc, axis=-1, keepdims=True)
            o = o + _bf16_dot(p_loc, v_s[krows, :], _NN)
        o_ref[0, qrows, :] = (o / den).astype(o_ref.dtype)
        return carry

    lax.fori_loop(0, n_rows // NA_ROWS, step, 0, unroll=NA_UNROLL)


def neighbourhood_attention(proj, proj_c, rpb, q_norm, k_norm, local):
    B, L, _ = proj.shape
    Lc = proj_c.shape[1]
    n_rows = L // GRID_W
    norm_spec = pl.BlockSpec((1, HEAD_DIM), lambda b, h: (0, 0))
    qn = q_norm.reshape(1, HEAD_DIM).astype(jnp.float32)
    kn = k_norm.reshape(1, HEAD_DIM).astype(jnp.float32)
    ctx_specs = [_col_spec(Lc, K_NK), _col_spec(Lc, K_NV)]
    ctx_scratch = [pltpu.VMEM((Lc, HEAD_DIM), jnp.bfloat16), pltpu.VMEM((Lc, HEAD_DIM), jnp.bfloat16)]
    assert n_rows % NA_ROWS == 0
    if local:
        assert n_rows >= NA_SPAN
        bw = na_bias_windows(rpb)
        in_specs = ([_col_spec(L, KV_COLS + R_NQ), _col_spec(L, K_NK), _col_spec(L, K_NV)] + ctx_specs
                    + [pl.BlockSpec((1,) + bw.shape[1:], lambda b, h: (h, 0, 0, 0)), norm_spec, norm_spec])
        args = [proj, proj, proj, proj_c, proj_c, bw, qn, kn]
        scratch = [pltpu.VMEM((L, HEAD_DIM), jnp.bfloat16), pltpu.VMEM((L, HEAD_DIM), jnp.bfloat16)] + ctx_scratch
    else:
        in_specs = [_col_spec(L, KV_COLS + R_NQ)] + ctx_specs + [norm_spec, norm_spec]
        args = [proj, proj_c, proj_c, qn, kn]
        scratch = ctx_scratch
    return pl.pallas_call(
        functools.partial(_na_kernel, n_rows=n_rows, local=local),
        grid=(B, NA_HEADS),
        in_specs=in_specs,
        out_specs=_col_spec(L, 0),
        out_shape=jax.ShapeDtypeStruct((B, L, NA_W), jnp.bfloat16),
        scratch_shapes=scratch,
        compiler_params=pltpu.CompilerParams(
            dimension_semantics=("parallel", "parallel"), vmem_limit_bytes=VMEM_LIMIT_BYTES),
        name="neighbourhood_attention",
    )(*args)


def dft_tables(L):
    n = 2 * L
    f = jnp.arange(L, dtype=jnp.int32)[:, None]
    t = jnp.arange(L, dtype=jnp.int32)[None, :]
    ang = ((f * t) % n).astype(jnp.float32) * (2.0 * math.pi / n)
    cm = jnp.cos(ang)
    nyquist = jnp.where(t % 2 == 0, 1.0, -1.0)
    sm = jnp.where(f == 0, nyquist, jnp.sin(ang))
    return cm.astype(jnp.bfloat16), sm.astype(jnp.bfloat16)


def hy_filter_spectrum(k, cm, sm):
    L = k.shape[0]
    kf = k[:, :, 0]
    kb = k[:, :, 1].at[0].set(0.0)
    ksum = (kf + kb).reshape(L, -1)
    kdiff = (kf - kb).reshape(L, -1)
    kc = mm(cm, ksum)
    ks = mm(sm, kdiff)
    sign = jnp.where(jnp.arange(L) % 2 == 0, 1.0, -1.0)[:, None]
    ks = ks.at[0].set(jnp.sum(sign * ksum, axis=0))
    split = lambda a: jnp.moveaxis(a.reshape(L, HY_ORDER, -1), 1, 0)
    return split(kc), split(ks)


def _hy_short_kernel(u_ref, w_ref, b_ref, o_ref):
    u = u_ref[0]
    L = u.shape[0]
    t = lax.broadcasted_iota(jnp.int32, u.shape, 0)
    prev = jnp.where(t == 0, 0.0, pltpu.roll(u, 1, axis=0))
    nxt = jnp.where(t == L - 1, 0.0, pltpu.roll(u, L - 1, axis=0))
    o_ref[0] = prev * w_ref[0:1, :] + u * w_ref[1:2, :] + nxt * w_ref[2:3, :] + b_ref[...]


HY_SHORT_COLS = 256


def hy_short(proj, conv_w, conv_b):
    B, L, _ = proj.shape
    cols = HY_SHORT_COLS
    n = 3 * HY_W
    return pl.pallas_call(
        _hy_short_kernel,
        grid=(B, n // cols),
        in_specs=[_col_spec(L, KV_COLS + R_HY, cols),
                  pl.BlockSpec((HY_SHORT, cols), lambda b, j: (0, j)),
                  pl.BlockSpec((1, cols), lambda b, j: (0, j))],
        out_specs=_col_spec(L, 0, cols),
        out_shape=jax.ShapeDtypeStruct((B, L, n), jnp.float32),
        compiler_params=pltpu.CompilerParams(
            dimension_semantics=("parallel", "parallel"), vmem_limit_bytes=VMEM_LIMIT_BYTES),
        name="hy_short",
    )(proj, conv_w.astype(jnp.float32), conv_b.reshape(1, n).astype(jnp.float32))


def _hy_fwd_kernel(cm_ref, sm_ref, u_ref, kc_ref, ks_ref, yc_ref, ys_ref):
    u = u_ref[0]
    uc = jnp.dot(cm_ref[...], u, preferred_element_type=jnp.float32)
    us = jnp.dot(sm_ref[...], u, preferred_element_type=jnp.float32)
    kc = kc_ref[...]
    ks = ks_ref[...]
    fb = uc.shape[0]
    f = pl.program_id(0) * fb + lax.broadcasted_iota(jnp.int32, (fb, 1), 0)
    packed = f == 0
    wgt = jnp.where(packed, 1.0, 2.0) / (2 * cm_ref.shape[1])
    yc_ref[0] = (jnp.where(packed, uc * kc, uc * kc - us * ks) * wgt).astype(yc_ref.dtype)
    ys_ref[0] = (jnp.where(packed, us * ks, uc * ks + us * kc) * wgt).astype(ys_ref.dtype)


HY_DFT_ROWS = 512


def hy_fwd(u, cm, sm, kc, ks):
    B, L, C = u.shape
    fb = min(HY_DFT_ROWS, L)
    mat_spec = pl.BlockSpec((fb, L), lambda f, b: (f, 0))
    k_spec = pl.BlockSpec((fb, C), lambda f, b: (f, 0))
    y_spec = pl.BlockSpec((1, fb, C), lambda f, b: (b, f, 0))
    y_shape = jax.ShapeDtypeStruct((B, L, C), jnp.bfloat16)
    return pl.pallas_call(
        _hy_fwd_kernel,
        grid=(L // fb, B),
        in_specs=[mat_spec, mat_spec, pl.BlockSpec((1, L, C), lambda f, b: (b, 0, 0)), k_spec, k_spec],
        out_specs=[y_spec, y_spec],
        out_shape=[y_shape, y_shape],
        compiler_params=pltpu.CompilerParams(
            dimension_semantics=("parallel", "parallel"), vmem_limit_bytes=VMEM_LIMIT_BYTES),
        name="hy_fwd",
    )(cm, sm, u, kc, ks)


def _hy_inv_kernel(cm_ref, sm_ref, yc_ref, ys_ref, u_ref, gate_ref, skip_ref, o_ref):
    conv = (lax.dot_general(cm_ref[...], yc_ref[0], (_TN, ((), ())), preferred_element_type=jnp.float32)
            + lax.dot_general(sm_ref[...], ys_ref[0], (_TN, ((), ())), preferred_element_type=jnp.float32))
    o_ref[0] = (gate_ref[0] * (conv + u_ref[0] * skip_ref[...])).astype(o_ref.dtype)


HY_INV_COLS = 512


def hy_inv(yc, ys, cm, sm, u, u_col0, gate, gate_col0, skip, out_dtype):
    B, L, C = yc.shape
    tb = min(HY_DFT_ROWS, L)
    cols = HY_INV_COLS
    once = pl.Buffered(1)
    mat_spec = pl.BlockSpec((L, tb), lambda b, j, t: (0, t))
    y_spec = pl.BlockSpec((1, L, cols), lambda b, j, t: (b, 0, j), pipeline_mode=once)

    def win(col0):
        return pl.BlockSpec((1, tb, cols), lambda b, j, t: (b, t, col0 // cols + j))

    return pl.pallas_call(
        _hy_inv_kernel,
        grid=(B, C // cols, L // tb),
        in_specs=[mat_spec, mat_spec, y_spec, y_spec, win(u_col0), win(gate_col0),
                  pl.BlockSpec((1, cols), lambda b, j, t: (0, j))],
        out_specs=win(0),
        out_shape=jax.ShapeDtypeStruct((B, L, C), out_dtype),
        compiler_params=pltpu.CompilerParams(
            dimension_semantics=("parallel", "parallel", "arbitrary"), vmem_limit_bytes=VMEM_LIMIT_BYTES),
        name="hy_inv",
    )(cm, sm, yc, ys, u, gate, skip.reshape(1, C).astype(jnp.float32))


def hyena(proj, conv_w, conv_b, filt, skip, tables):
    B, L, _ = proj.shape
    cm, sm = tables
    hyu = hy_short(proj, conv_w, conv_b)
    kc, ks = hy_filter_spectrum(hyena_filters(L, *filt), cm, sm)
    v = hyu[..., :HY_W].astype(jnp.bfloat16)
    yc, ys = hy_fwd(v, cm, sm, kc[0], ks[0])
    z1 = hy_inv(yc, ys, cm, sm, hyu, 0, hyu, HY_W, skip[0], jnp.float32)
    yc, ys = hy_fwd(z1.astype(jnp.bfloat16), cm, sm, kc[1], ks[1])
    return hy_inv(yc, ys, cm, sm, z1, 0, hyu, 2 * HY_W, skip[1], jnp.bfloat16)


def mixer(proj, proj_c, p, log_f, log_b, s_f, s_b, tables, latent):
    hy = hyena(proj, p['hy_conv_w'], p['hy_conv_b'], p['hy_f'], p['hy_skip'], tables)
    ro, sf, sb = retention(proj, log_f, log_b, s_f, s_b, rope=latent)
    no = neighbourhood_attention(proj, proj_c, p['na_rpb'], p['q_norm'], p['k_norm'], local=latent)
    return mm(jnp.concatenate([hy, ro, no], axis=-1), p['w_out']), sf, sb


def _norm_mod_kernel(*refs, residual):
    if residual:
        x_ref, g_ref, shift_ref, scale_ref, res_ref, gate_ref, xo_ref, h_ref = refs
        x = x_ref[0] + gate_ref[0] * res_ref[0]
        xo_ref[0] = x
    else:
        x_ref, g_ref, shift_ref, scale_ref, h_ref = refs
        x = x_ref[0]
    y = x * lax.rsqrt(jnp.mean(x * x, axis=-1, keepdims=True) + EPS) * g_ref[...]
    h_ref[0] = (y * (1.0 + scale_ref[0]) + shift_ref[0]).astype(h_ref.dtype)


NORM_ROWS = 256


def norm_mod(x, g, shift, scale, res=None, gate=None):
    B, L, D = x.shape
    rb = min(NORM_ROWS, L)
    residual = res is not None

    def vec_spec(v):
        per_batch = v.shape[0] > 1
        return pl.BlockSpec((1, 1, D), lambda b, r: (b if per_batch else 0, 0, 0))

    row_spec = pl.BlockSpec((1, rb, D), lambda b, r: (b, r, 0))
    in_specs = [row_spec, pl.BlockSpec((1, D), lambda b, r: (0, 0)), vec_spec(shift), vec_spec(scale)]
    args = [x, g.reshape(1, D).astype(jnp.float32), shift, scale]
    h_shape = jax.ShapeDtypeStruct((B, L, D), jnp.bfloat16)
    if residual:
        in_specs += [row_spec, vec_spec(gate)]
        args += [res, gate]
        out_specs, out_shape = [row_spec, row_spec], [jax.ShapeDtypeStruct((B, L, D), jnp.float32), h_shape]
    else:
        out_specs, out_shape = row_spec, h_shape
    return pl.pallas_call(
        functools.partial(_norm_mod_kernel, residual=residual),
        grid=(B, L // rb),
        in_specs=in_specs,
        out_specs=out_specs,
        out_shape=out_shape,
        compiler_params=pltpu.CompilerParams(
            dimension_semantics=("parallel", "parallel"), vmem_limit_bytes=VMEM_LIMIT_BYTES),
        name="norm_mod",
    )(*args)


def layer(xl, xc, pend_l, pend_c, mod_l, mod_c, p, last, tables_l, tables_c):
    sa_l, ca_l, ga_l, sf_l, cf_l, gf_l = jnp.split(mod_l, N_MOD, axis=-1)
    mod_c_parts = jnp.split(mod_c, mod_c.shape[-1] // D_MODEL, axis=-1)
    sa_c, ca_c = mod_c_parts[0], mod_c_parts[1]
    log_f = -jnp.exp(p['ret_decay'][0].astype(jnp.float32))
    log_b = -jnp.exp(p['ret_decay'][1].astype(jnp.float32))
    w_in = p['w_in'].astype(jnp.bfloat16)
    if pend_c is None:
        hc = norm_mod(xc, p['norm_mix'], sa_c, ca_c)
    else:
        xc, hc = norm_mod(xc, p['norm_mix'], sa_c, ca_c, *pend_c)
    proj_c = mm(hc, w_in)
    zeros = jnp.zeros((xc.shape[0], RET_HEADS, HEAD_DIM, HEAD_DIM), jnp.float32)
    if last:
        _, s_f, s_b = retention(proj_c, log_f, log_b, zeros, zeros, rope=False)
    else:
        y_c, s_f, s_b = mixer(proj_c, proj_c, p, log_f, log_b, zeros, zeros, tables_c, False)
    if pend_l is None:
        hl = norm_mod(xl, p['norm_mix'], sa_l, ca_l)
    else:
        xl, hl = norm_mod(xl, p['norm_mix'], sa_l, ca_l, *pend_l)
    y_l, _, _ = mixer(mm(hl, w_in), proj_c, p, log_f, log_b, s_f, s_b, tables_l, True)
    xl, hf_l = norm_mod(xl, p['norm_ffn'], sf_l, cf_l, y_l, ga_l)
    pend_l = (peer(hf_l, p['peer_wq'], p['peer_keys'], p['peer_u'], p['peer_v']), gf_l)
    if last:
        return xl, None, pend_l, None
    ga_c, sf_c, cf_c, gf_c = mod_c_parts[2], mod_c_parts[3], mod_c_parts[4], mod_c_parts[5]
    xc, hf_c = norm_mod(xc, p['norm_ffn'], sf_c, cf_c, y_c, ga_c)
    pend_c = (peer(hf_c, p['peer_wq'], p['peer_keys'], p['peer_u'], p['peer_v']), gf_c)
    return xl, xc, pend_l, pend_c


def kernel(x, c, ctx, c_ctx, w_mod, b_mod, norm_mix, w_in, hy_conv_w, hy_conv_b,
           hy_f_w1, hy_f_b1, hy_f_freq, hy_f_w2, hy_f_b2, hy_f_w3, hy_f_b3, hy_f_w4,
           hy_skip, ret_decay, na_rpb, q_norm, k_norm, w_out, norm_ffn,
           peer_wq, peer_keys, peer_u, peer_v):
    xl, xc = x, ctx
    sc = jax.nn.silu(c)
    scc = jax.nn.silu(c_ctx)
    tables_l = dft_tables(x.shape[1])
    tables_c = dft_tables(ctx.shape[1])
    pend_l = pend_c = None
    for l in range(DEPTH):
        last = l == DEPTH - 1
        p = {
            'norm_mix': norm_mix[l], 'w_in': w_in[l],
            'hy_conv_w': hy_conv_w[l], 'hy_conv_b': hy_conv_b[l],
            'hy_f': (hy_f_w1[l], hy_f_b1[l], hy_f_freq[l], hy_f_w2[l], hy_f_b2[l],
                     hy_f_w3[l], hy_f_b3[l], hy_f_w4[l]),
            'hy_skip': hy_skip[l], 'ret_decay': ret_decay[l], 'na_rpb': na_rpb[l],
            'q_norm': q_norm[l], 'k_norm': k_norm[l], 'w_out': w_out[l],
            'norm_ffn': norm_ffn[l], 'peer_wq': peer_wq[l], 'peer_keys': peer_keys[l],
            'peer_u': peer_u[l].astype(jnp.bfloat16), 'peer_v': peer_v[l].astype(jnp.bfloat16),
        }
        mod_l = (sc @ w_mod[l] + b_mod[l])[:, None, :]
        n_ctx = 2 if last else N_MOD
        mod_c = (scc @ w_mod[l][:, :n_ctx * D_MODEL] + b_mod[l][:n_ctx * D_MODEL])[None, None, :]
        xl, xc, pend_l, pend_c = layer(xl, xc, pend_l, pend_c, mod_l, mod_c, p, last, tables_l, tables_c)
    update, gate = pend_l
    return xl + gate * update
```

```python
import functools
import math

import jax
import jax.numpy as jnp
from jax import lax
from jax.experimental import pallas as pl
from jax.experimental.pallas import tpu as pltpu

D_MODEL = 4096
BATCH = 4
SEQ = 4096
DEPTH = 2

GRID_W = 64
CTX_LEN = 256
HEAD_DIM = 128
N_MIX_HEADS = D_MODEL // HEAD_DIM
HY_GROUPS = N_MIX_HEADS // 4
RET_HEADS = (N_MIX_HEADS - HY_GROUPS) // 2
NA_HEADS = N_MIX_HEADS - HY_GROUPS - RET_HEADS
HY_W = HY_GROUPS * HEAD_DIM
RET_W = RET_HEADS * HEAD_DIM
NA_W = NA_HEADS * HEAD_DIM
K_RK = 0
K_RV = K_RK + RET_W
K_NK = K_RV + RET_W
K_NV = K_NK + NA_W
KV_COLS = K_NV + NA_W
R_HY = 0
R_RQ = R_HY + 3 * HY_W
R_RG = R_RQ + RET_W
R_NQ = R_RG + RET_W
REST_COLS = R_NQ + NA_W
IN_COLS = KV_COLS + REST_COLS
N_MOD = 6
HY_ORDER = 2
HY_SHORT = 3
HY_EMB = 33
HY_BANDS = (HY_EMB - 1) // 2
HY_FILTER_HIDDEN = 64
HY_DECAY_TARGET = 1e-2
HY_SHORT_DECAY_PCT = 0.3
HY_LONG_DECAY_PCT = 1.5
HY_MAX_DECAY = math.log(HY_DECAY_TARGET) / HY_SHORT_DECAY_PCT
HY_MIN_DECAY = math.log(HY_DECAY_TARGET) / HY_LONG_DECAY_PCT
HY_FILTER_SCALE = 0.05
RET_CHUNK = 128
RET_SCALE = HEAD_DIM ** -0.5
ROPE_BASE = 10000.0
NA_KH = 8
NA_KW = 16
PEER_HEADS = 8
PEER_NKEYS = 128
PEER_N = PEER_NKEYS * PEER_NKEYS
PEER_TOPK = 16
PEER_DK = 256
PEER_BLOCK = 64
EPS = 1e-6
NEG_INF = -1e30

VMEM_LIMIT_BYTES = 56 * 1024 * 1024


def _mm_kernel(a_ref, b_ref, o_ref):
    o_ref[...] = jnp.dot(a_ref[...], b_ref[...], preferred_element_type=jnp.float32)


def _pick_block(n, candidates):
    for c in candidates:
        if n % c == 0:
            return c
    return n


def mm(a, b):
    lead = a.shape[:-1]
    K = a.shape[-1]
    a2 = a.reshape(-1, K).astype(jnp.bfloat16)
    b2 = b.astype(jnp.bfloat16)
    M, N = a2.shape[0], b2.shape[1]
    bm = _pick_block(M, (1024, 512, 256, 128, 64, 32, 16, 8))
    bn = _pick_block(N, (1024, 768, 512, 256, 128))
    out = pl.pallas_call(
        _mm_kernel,
        grid=(M // bm, N // bn),
        in_specs=[pl.BlockSpec((bm, K), lambda i, j: (i, 0)),
                  pl.BlockSpec((K, bn), lambda i, j: (0, j))],
        out_specs=pl.BlockSpec((bm, bn), lambda i, j: (i, j)),
        out_shape=jax.ShapeDtypeStruct((M, N), jnp.float32),
        compiler_params=pltpu.CompilerParams(
            dimension_semantics=("parallel", "parallel"),
            vmem_limit_bytes=VMEM_LIMIT_BYTES),
        name="mm",
    )(a2, b2)
    return out.reshape(lead + (N,))


def hyena_filters(L, w1, b1, freq, w2, b2, w3, b3, w4):
    f32 = lambda a: a.astype(jnp.float32)
    t = jnp.linspace(0.0, 1.0, L, dtype=jnp.float32)[:, None]
    w = 2.0 * math.pi * jnp.arange(L, dtype=jnp.float32)[:, None] / L
    f = jnp.linspace(1e-4, HY_BANDS - 1, HY_BANDS, dtype=jnp.float32)[None, :]
    z = jnp.concatenate([t, jnp.cos(f * w), -jnp.sin(f * w)], axis=-1)
    h = jnp.sin(f32(freq) * (z @ f32(w1) + f32(b1)))
    h = jnp.sin(f32(freq) * (h @ f32(w2) + f32(b2)))
    h = jnp.sin(f32(freq) * (h @ f32(w3) + f32(b3)))
    k = (h @ f32(w4)).reshape(L, HY_ORDER, 2, HY_W)
    deltas = jnp.abs(jnp.linspace(HY_MIN_DECAY, HY_MAX_DECAY, HY_W, dtype=jnp.float32))
    return k * jnp.exp(-t[:, :, None, None] * deltas)


def _mm_nt_kernel(w_ref, x_ref, o_ref):
    o_ref[...] = lax.dot_general(w_ref[...], x_ref[...], (((1,), (1,)), ((), ())),
                                 preferred_element_type=jnp.float32)


def mm_nt(w, x):
    N, K = w.shape
    M = x.shape[0]
    bn = _pick_block(N, (1024, 512, 256, 128))
    bm = _pick_block(M, (1024, 512, 256, 128))
    return pl.pallas_call(
        _mm_nt_kernel,
        grid=(M // bm, N // bn),
        in_specs=[pl.BlockSpec((bn, K), lambda i, j: (j, 0)),
                  pl.BlockSpec((bm, K), lambda i, j: (i, 0))],
        out_specs=pl.BlockSpec((bn, bm), lambda i, j: (j, i)),
        out_shape=jax.ShapeDtypeStruct((N, M), jnp.float32),
        compiler_params=pltpu.CompilerParams(
            dimension_semantics=("parallel", "parallel"),
            vmem_limit_bytes=VMEM_LIMIT_BYTES),
        name="mm_nt",
    )(w.astype(jnp.bfloat16), x.astype(jnp.bfloat16))


def _top_values(s, k):
    vals = []
    work = s
    for _ in range(k):
        m = jnp.max(work, axis=0, keepdims=True)
        vals.append(m)
        work = jnp.where(work == m, -jnp.inf, work)
    return jnp.concatenate(vals, axis=0)


def _peer_route_kernel(qt_ref, keys_ref, s1_ref, b_ref, thr_ref, a_ref):
    half = PEER_DK // 2
    K = PEER_TOPK
    for h in range(PEER_HEADS):
        s = []
        for p in range(2):
            q = qt_ref[pl.ds((2 * h + p) * half, half), :]
            s.append(jnp.dot(keys_ref[2 * h + p], q, preferred_element_type=jnp.float32,
                             precision=lax.Precision.HIGHEST))
        s0, s1 = s
        v0 = _top_values(s0, K)
        v1 = _top_values(s1, K)
        v0_tail = jnp.where(lax.broadcasted_iota(jnp.int32, v0.shape, 0) < 4, -jnp.inf, v0)
        cand = jnp.concatenate([v0[0:1] + v1] + [v0[a:a + 1] + v1[0:8] for a in (1, 2, 3)]
                               + [v1[b:b + 1] + v0_tail for b in (0, 1, 2)], axis=0)
        tau = _top_values(cand, K)[K - 1:K]
        m = v0[0:1] + v1[0:1]
        z = jnp.sum(jnp.where(cand >= tau, jnp.exp(cand - m), 0.0), axis=0, keepdims=True)
        thr = jnp.full_like(s0, jnp.inf)
        for b in range(K):
            vb = v1[b:b + 1]
            thr = jnp.minimum(thr, jnp.where(s0 + vb >= tau, vb, jnp.inf))
        s1_ref[h] = s1
        b_ref[h] = jnp.exp(s1 - v1[0:1])
        thr_ref[h] = thr
        a_ref[h] = jnp.exp(s0 - v0[0:1]) / z


PEER_ROUTE_LANES = 128


def peer_route(qt, sub_keys):
    T = qt.shape[1]
    keys = sub_keys.reshape(PEER_HEADS * 2, PEER_NKEYS, PEER_DK // 2).astype(jnp.float32)
    bt = PEER_ROUTE_LANES
    out = jax.ShapeDtypeStruct((PEER_HEADS, PEER_NKEYS, T), jnp.float32)
    ospec = pl.BlockSpec((PEER_HEADS, PEER_NKEYS, bt), lambda i: (0, 0, i))
    return pl.pallas_call(
        _peer_route_kernel,
        grid=(T // bt,),
        in_specs=[pl.BlockSpec((PEER_HEADS * PEER_DK, bt), lambda i: (0, i)),
                  pl.BlockSpec(keys.shape, lambda i: (0, 0, 0))],
        out_specs=[ospec] * 4,
        out_shape=[out] * 4,
        compiler_params=pltpu.CompilerParams(
            dimension_semantics=("parallel",), vmem_limit_bytes=VMEM_LIMIT_BYTES),
        name="peer_route",
    )(qt, keys)


PEER_DENSE_TOKENS = 512
PEER_DENSE_GROUPS = 4
PEER_LANE_CHUNK = 128
PEER_SPLIT_V = False
PEER_U_PARTS = 1


def _peer_dense_kernel(x_ref, u_ref, v_ref, s1_ref, b_ref, thr_ref, a_ref, o_ref, thr_s, a_s, pre_s, act_s, *, groups):
    e = pl.program_id(1)

    @pl.when(e == 0)
    def _():
        o_ref[...] = jnp.zeros_like(o_ref)

    for g in range(groups):
        for h in range(PEER_HEADS):
            thr_s[g * PEER_HEADS + h] = thr_ref[h, pl.ds(e * groups + g, 1), :]
            a_s[g * PEER_HEADS + h] = a_ref[h, pl.ds(e * groups + g, 1), :]

    half_groups = groups // 2
    half_rows = half_groups * PEER_NKEYS
    for part in range(PEER_U_PARTS):
        rows = pl.ds(part * (2 * half_rows // PEER_U_PARTS), 2 * half_rows // PEER_U_PARTS)
        pre_s[rows, :] = lax.dot_general(u_ref[rows, :], x_ref[...], (((1,), (1,)), ((), ())),
                                         preferred_element_type=jnp.float32)
    for half in range(2):
        for g in range(half_groups):
            gi = half * half_groups + g
            rows_g = pl.ds(gi * PEER_NKEYS, PEER_NKEYS)
            for c in range(x_ref.shape[0] // PEER_LANE_CHUNK):
                lanes = pl.ds(c * PEER_LANE_CHUNK, PEER_LANE_CHUNK)
                w = jnp.zeros((PEER_NKEYS, PEER_LANE_CHUNK), jnp.float32)
                for h in range(PEER_HEADS):
                    r = gi * PEER_HEADS + h
                    w = w + jnp.where(s1_ref[h, :, lanes] >= thr_s[r, :, lanes], b_ref[h, :, lanes], 0.0) * a_s[r, :, lanes]
                act_s[rows_g, lanes] = (jax.nn.gelu(pre_s[rows_g, lanes]) * w).astype(jnp.bfloat16)
        if PEER_SPLIT_V:
            rows = pl.ds(half * half_rows, half_rows)
            o_ref[...] += lax.dot_general(act_s[rows, :], v_ref[rows, :], (((0,), (0,)), ((), ())),
                                          preferred_element_type=jnp.float32)
    if not PEER_SPLIT_V:
        o_ref[...] += lax.dot_general(act_s[...], v_ref[...], (((0,), (0,)), ((), ())),
                                      preferred_element_type=jnp.float32)


def peer_dense(x, u, v, s1, b, thr, a):
    T, D = x.shape
    bt = min(PEER_DENSE_TOKENS, T)
    groups = PEER_DENSE_GROUPS
    be = groups * PEER_NKEYS
    once = pl.Buffered(1)
    rspec = pl.BlockSpec((PEER_HEADS, PEER_NKEYS, bt), lambda t, e: (0, 0, t), pipeline_mode=once)
    return pl.pallas_call(
        functools.partial(_peer_dense_kernel, groups=groups),
        grid=(T // bt, PEER_N // be),
        in_specs=[pl.BlockSpec((bt, D), lambda t, e: (t, 0), pipeline_mode=once),
                  pl.BlockSpec((be, D), lambda t, e: (e, 0)),
                  pl.BlockSpec((be, D), lambda t, e: (e, 0)),
                  rspec, rspec, rspec, rspec],
        out_specs=pl.BlockSpec((bt, D), lambda t, e: (t, 0)),
        out_shape=jax.ShapeDtypeStruct((T, D), jnp.float32),
        scratch_shapes=[pltpu.VMEM((groups * PEER_HEADS, 1, bt), jnp.float32),
                        pltpu.VMEM((groups * PEER_HEADS, 1, bt), jnp.float32),
                        pltpu.VMEM((be, bt), jnp.float32),
                        pltpu.VMEM((be, bt), jnp.bfloat16)],
        compiler_params=pltpu.CompilerParams(
            dimension_semantics=("parallel", "arbitrary"),
            vmem_limit_bytes=VMEM_LIMIT_BYTES),
        name="peer_dense",
    )(x, u, v, s1, b, thr, a)


def peer(h, w_q, sub_keys, exp_u, exp_v):
    B, L, D = h.shape
    x = h.reshape(B * L, D).astype(jnp.bfloat16)
    qt = mm_nt(w_q.T, x)
    s1, b, thr, a = peer_route(qt, sub_keys)
    return peer_dense(x, exp_u, exp_v, s1, b, thr, a).reshape(B, L, D)


def _col_spec(L, col0, width=HEAD_DIM):
    assert col0 % width == 0
    return pl.BlockSpec((1, L, width), lambda b, j: (b, 0, col0 // width + j))


def _bf16_dot(a, b, dims):
    return lax.dot_general(a.astype(jnp.bfloat16), b.astype(jnp.bfloat16), (dims, ((), ())),
                           preferred_element_type=jnp.float32)


_NN = ((1,), (0,))
_NT = ((1,), (1,))
_TN = ((0,), (0,))


def rope_tables(L):
    quarter = HEAD_DIM // 4
    t = jnp.arange(L)
    lane = jnp.arange(HEAD_DIM)
    pos = jnp.where(lane[None, :] < 2 * quarter, (t // GRID_W)[:, None], (t % GRID_W)[:, None]).astype(jnp.float32)
    inv = ROPE_BASE ** (-(lane % quarter).astype(jnp.float32) / quarter)
    ang = pos * inv[None, :]
    sign = jnp.where((lane % (2 * quarter)) < quarter, -1.0, 1.0)
    return jnp.cos(ang), jnp.sin(ang) * sign[None, :]


def _rope(x, cos, sin_signed):
    quarter = HEAD_DIM // 4
    lane = lax.broadcasted_iota(jnp.int32, x.shape, 1)
    partner = jnp.where((lane % (2 * quarter)) < quarter,
                        pltpu.roll(x, HEAD_DIM - quarter, axis=1), pltpu.roll(x, quarter, axis=1))
    return x * cos + partner * sin_signed


def _ret_kernel(*refs, n_chunks, rope):
    if rope:
        (q_ref, k_ref, v_ref, g_ref, sf0_ref, sb0_ref, dec_ref, cos_ref, sin_ref,
         o_ref, sf_ref, sb_ref, kvf, kvb, kr) = refs
    else:
        (q_ref, k_ref, v_ref, g_ref, sf0_ref, sb0_ref, dec_ref,
         o_ref, sf_ref, sb_ref, kvf, kvb, kr) = refs
    C = RET_CHUNK
    lf = dec_ref[0, 0:1, :]
    lb = dec_ref[0, 1:2, :]
    pos = lax.broadcasted_iota(jnp.int32, (C, HEAD_DIM), 0).astype(jnp.float32)
    kdf = jnp.exp(lf * (C - 1.0 - pos))
    kdb = jnp.exp(lb * pos)
    qdf = jnp.exp(lf * (pos + 1.0))
    qdb = jnp.exp(lb * (C - pos))
    cf = jnp.exp(lf * C)
    cb = jnp.exp(lb * C)
    diff = (lax.broadcasted_iota(jnp.int32, (C, C), 0) - lax.broadcasted_iota(jnp.int32, (C, C), 1)).astype(jnp.float32)
    dmat = (jnp.where(diff >= 0, jnp.exp(lf * jnp.maximum(diff, 0.0)), 0.0)
            + jnp.where(diff <= 0, jnp.exp(lb * jnp.maximum(-diff, 0.0)), 0.0))

    def rows(m):
        return pl.ds(pl.multiple_of(m * C, C), C)

    def summaries(m, carry):
        k = k_ref[0, rows(m), :] * RET_SCALE
        if rope:
            k = _rope(k, cos_ref[rows(m), :], sin_ref[rows(m), :])
        kr[rows(m), :] = k
        v = v_ref[0, rows(m), :]
        kvf[m] = _bf16_dot(k * kdf, v, _TN)
        kvb[m] = _bf16_dot(k * kdb, v, _TN)
        return carry

    lax.fori_loop(0, n_chunks, summaries, 0, unroll=4)

    def scan_f(m, s):
        nxt = s * cf + kvf[m]
        kvf[m] = s
        return nxt

    sf_ref[0, 0] = lax.fori_loop(0, n_chunks, scan_f, sf0_ref[0, 0])

    def scan_b(i, s):
        m = n_chunks - 1 - i
        nxt = s * cb + kvb[m]
        kvb[m] = s
        return nxt

    sb_ref[0, 0] = lax.fori_loop(0, n_chunks, scan_b, sb0_ref[0, 0])

    def outputs(m, carry):
        q = q_ref[0, rows(m), :]
        if rope:
            q = _rope(q, cos_ref[rows(m), :], sin_ref[rows(m), :])
        k = kr[rows(m), :]
        v = v_ref[0, rows(m), :]
        a = _bf16_dot(q, k, _NT) * dmat
        o = (_bf16_dot(a, v, _NN) + _bf16_dot(q * qdf, kvf[m], _NN) + _bf16_dot(q * qdb, kvb[m], _NN))
        o = o * lax.rsqrt(jnp.mean(o * o, axis=-1, keepdims=True) + EPS)
        o_ref[0, rows(m), :] = (o * jax.nn.silu(g_ref[0, rows(m), :])).astype(o_ref.dtype)
        return carry

    lax.fori_loop(0, n_chunks, outputs, 0, unroll=4)


def retention(proj, log_f, log_b, s_f, s_b, rope):
    B, L, _ = proj.shape
    n_chunks = L // RET_CHUNK
    dec = jnp.zeros((RET_HEADS, 8, HEAD_DIM), jnp.float32)
    dec = dec.at[:, 0, :].set(log_f[:, None]).at[:, 1, :].set(log_b[:, None])
    state_spec = pl.BlockSpec((1, 1, HEAD_DIM, HEAD_DIM), lambda b, h: (b, h, 0, 0))
    in_specs = [_col_spec(L, KV_COLS + R_RQ), _col_spec(L, K_RK), _col_spec(L, K_RV), _col_spec(L, KV_COLS + R_RG),
                state_spec, state_spec, pl.BlockSpec((1, 8, HEAD_DIM), lambda b, h: (h, 0, 0))]
    args = [proj, proj, proj, proj, s_f, s_b, dec]
    if rope:
        tab_spec = pl.BlockSpec((L, HEAD_DIM), lambda b, h: (0, 0))
        in_specs += [tab_spec, tab_spec]
        args += list(rope_tables(L))
    state_shape = jax.ShapeDtypeStruct((B, RET_HEADS, HEAD_DIM, HEAD_DIM), jnp.float32)
    return pl.pallas_call(
        functools.partial(_ret_kernel, n_chunks=n_chunks, rope=rope),
        grid=(B, RET_HEADS),
        in_specs=in_specs,
        out_specs=[_col_spec(L, 0), state_spec, state_spec],
        out_shape=[jax.ShapeDtypeStruct((B, L, RET_W), jnp.bfloat16), state_shape, state_shape],
        scratch_shapes=[pltpu.VMEM((n_chunks, HEAD_DIM, HEAD_DIM), jnp.float32),
                        pltpu.VMEM((n_chunks, HEAD_DIM, HEAD_DIM), jnp.float32),
                        pltpu.VMEM((L, HEAD_DIM), jnp.float32)],
        compiler_params=pltpu.CompilerParams(
            dimension_semantics=("parallel", "parallel"), vmem_limit_bytes=VMEM_LIMIT_BYTES),
        name="retention",
    )(*args)


def na_bias_windows(rpb):
    col = jnp.arange(GRID_W)
    cstart = jnp.clip(col - NA_KW // 2, 0, GRID_W - NA_KW)
    cmask = (col[None, :] >= cstart[:, None]) & (col[None, :] < cstart[:, None] + NA_KW)
    cidx = jnp.clip(col[None, :] - col[:, None] + NA_KW - 1, 0, 2 * NA_KW - 2)
    rpb_c = jnp.where(cmask[None, None], rpb[:, :, cidx].astype(jnp.float32), NEG_INF)
    masked = jnp.full_like(rpb_c[:, 0], NEG_INF)
    tabs = []
    for w in range(NA_KH):
        for off in range(NA_OFFS):
            tabs.append(jnp.concatenate(
                [rpb_c[:, w + u - off] if 0 <= u - off < NA_KH else masked for u in range(NA_SPAN)], axis=-1))
    return jnp.stack(tabs, axis=1)


NA_ROWS = 2
NA_SPAN = 10
NA_OFFS = NA_SPAN - NA_KH + 1
NA_UNROLL = 4


def _na_kernel(*refs, n_rows, local):
    if local:
        (q_ref, k_ref, v_ref, kc_ref, vc_ref, bw_ref, qn_ref, kn_ref, o_ref, k_s, v_s, kc_s, vc_s) = refs
    else:
        (q_ref, kc_ref, vc_ref, qn_ref, kn_ref, o_ref, kc_s, vc_s) = refs
    scale = HEAD_DIM ** -0.5

    def norm(x, g_ref):
        return x * lax.rsqrt(jnp.mean(x * x, axis=-1, keepdims=True) + EPS) * g_ref[...]

    kc_s[...] = norm(kc_ref[0], kn_ref).astype(jnp.bfloat16)
    vc_s[...] = vc_ref[0].astype(jnp.bfloat16)
    if local:
        k_s[...] = norm(k_ref[0], kn_ref).astype(jnp.bfloat16)
        v_s[...] = v_ref[0].astype(jnp.bfloat16)

    def step(i, carry):
        r0 = i * NA_ROWS
        qrows = pl.ds(pl.multiple_of(r0 * GRID_W, NA_ROWS * GRID_W), NA_ROWS * GRID_W)
        q = norm(q_ref[0, qrows, :], qn_ref).astype(jnp.bfloat16)
        s_ctx = _bf16_dot(q, kc_s[...], _NT) * scale
        m = jnp.max(s_ctx, axis=-1, keepdims=True)
        if local:
            span = jnp.clip(r0 - NA_KH // 2, 0, n_rows - NA_SPAN)
            krows = pl.ds(pl.multiple_of(span * GRID_W, GRID_W), NA_SPAN * GRID_W)
            bias = []
            for j in range(NA_ROWS):
                r = r0 + j
                start = jnp.clip(r - NA_KH // 2, 0, n_rows - NA_KH)
                bias.append(bw_ref[0, (start - r + NA_KH - 1) * NA_OFFS + (start - span)])
            s_loc = _bf16_dot(q, k_s[krows, :], _NT) * scale + jnp.concatenate(bias, axis=0)
            m = jnp.maximum(m, jnp.max(s_loc, axis=-1, keepdims=True))
            p_loc = jnp.exp(s_loc - m)
        p_ctx = jnp.exp(s_ctx - m)
        den = jnp.sum(p_ctx, axis=-1, keepdims=True)
        o = _bf16_dot(p_ctx, vc_s[...], _NN)
        if local:
            den = den + jnp.sum(p_loc, axis=-1, keepdims=True)
            o = o + _bf16_dot(p_loc, v_s[krows, :], _NN)
        o_ref[0, qrows, :] = (o / den).astype(o_ref.dtype)
        return carry

    lax.fori_loop(0, n_rows // NA_ROWS, step, 0, unroll=NA_UNROLL)


def neighbourhood_attention(proj, proj_c, rpb, q_norm, k_norm, local):
    B, L, _ = proj.shape
    Lc = proj_c.shape[1]
    n_rows = L // GRID_W
    norm_spec = pl.BlockSpec((1, HEAD_DIM), lambda b, h: (0, 0))
    qn = q_norm.reshape(1, HEAD_DIM).astype(jnp.float32)
    kn = k_norm.reshape(1, HEAD_DIM).astype(jnp.float32)
    ctx_specs = [_col_spec(Lc, K_NK), _col_spec(Lc, K_NV)]
    ctx_scratch = [pltpu.VMEM((Lc, HEAD_DIM), jnp.bfloat16), pltpu.VMEM((Lc, HEAD_DIM), jnp.bfloat16)]
    assert n_rows % NA_ROWS == 0
    if local:
        assert n_rows >= NA_SPAN
        bw = na_bias_windows(rpb)
        in_specs = ([_col_spec(L, KV_COLS + R_NQ), _col_spec(L, K_NK), _col_spec(L, K_NV)] + ctx_specs
                    + [pl.BlockSpec((1,) + bw.shape[1:], lambda b, h: (h, 0, 0, 0)), norm_spec, norm_spec])
        args = [proj, proj, proj, proj_c, proj_c, bw, qn, kn]
        scratch = [pltpu.VMEM((L, HEAD_DIM), jnp.bfloat16), pltpu.VMEM((L, HEAD_DIM), jnp.bfloat16)] + ctx_scratch
    else:
        in_specs = [_col_spec(L, KV_COLS + R_NQ)] + ctx_specs + [norm_spec, norm_spec]
        args = [proj, proj_c, proj_c, qn, kn]
        scratch = ctx_scratch
    return pl.pallas_call(
        functools.partial(_na_kernel, n_rows=n_rows, local=local),
        grid=(B, NA_HEADS),
        in_specs=in_specs,
        out_specs=_col_spec(L, 0),
        out_shape=jax.ShapeDtypeStruct((B, L, NA_W), jnp.bfloat16),
        scratch_shapes=scratch,
        compiler_params=pltpu.CompilerParams(
            dimension_semantics=("parallel", "parallel"), vmem_limit_bytes=VMEM_LIMIT_BYTES),
        name="neighbourhood_attention",
    )(*args)


def dft_tables(L):
    n = 2 * L
    f = jnp.arange(L, dtype=jnp.int32)[:, None]
    t = jnp.arange(L, dtype=jnp.int32)[None, :]
    ang = ((f * t) % n).astype(jnp.float32) * (2.0 * math.pi / n)
    cm = jnp.cos(ang)
    nyquist = jnp.where(t % 2 == 0, 1.0, -1.0)
    sm = jnp.where(f == 0, nyquist, jnp.sin(ang))
    return cm.astype(jnp.bfloat16), sm.astype(jnp.bfloat16)


def hy_filter_spectrum(k, cm, sm):
    L = k.shape[0]
    kf = k[:, :, 0]
    kb = k[:, :, 1].at[0].set(0.0)
    ksum = (kf + kb).reshape(L, -1)
    kdiff = (kf - kb).reshape(L, -1)
    kc = mm(cm, ksum)
    ks = mm(sm, kdiff)
    sign = jnp.where(jnp.arange(L) % 2 == 0, 1.0, -1.0)[:, None]
    ks = ks.at[0].set(jnp.sum(sign * ksum, axis=0))
    split = lambda a: jnp.moveaxis(a.reshape(L, HY_ORDER, -1), 1, 0)
    return split(kc), split(ks)


def _hy_short_kernel(u_ref, w_ref, b_ref, o_ref):
    u = u_ref[0]
    L = u.shape[0]
    t = lax.broadcasted_iota(jnp.int32, u.shape, 0)
    prev = jnp.where(t == 0, 0.0, pltpu.roll(u, 1, axis=0))
    nxt = jnp.where(t == L - 1, 0.0, pltpu.roll(u, L - 1, axis=0))
    o_ref[0] = prev * w_ref[0:1, :] + u * w_ref[1:2, :] + nxt * w_ref[2:3, :] + b_ref[...]


HY_SHORT_COLS = 256


def hy_short(proj, conv_w, conv_b):
    B, L, _ = proj.shape
    cols = HY_SHORT_COLS
    n = 3 * HY_W
    return pl.pallas_call(
        _hy_short_kernel,
        grid=(B, n // cols),
        in_specs=[_col_spec(L, KV_COLS + R_HY, cols),
                  pl.BlockSpec((HY_SHORT, cols), lambda b, j: (0, j)),
                  pl.BlockSpec((1, cols), lambda b, j: (0, j))],
        out_specs=_col_spec(L, 0, cols),
        out_shape=jax.ShapeDtypeStruct((B, L, n), jnp.float32),
        compiler_params=pltpu.CompilerParams(
            dimension_semantics=("parallel", "parallel"), vmem_limit_bytes=VMEM_LIMIT_BYTES),
        name="hy_short",
    )(proj, conv_w.astype(jnp.float32), conv_b.reshape(1, n).astype(jnp.float32))


def _hy_fwd_kernel(cm_ref, sm_ref, u_ref, kc_ref, ks_ref, yc_ref, ys_ref):
    u = u_ref[0]
    uc = jnp.dot(cm_ref[...], u, preferred_element_type=jnp.float32)
    us = jnp.dot(sm_ref[...], u, preferred_element_type=jnp.float32)
    kc = kc_ref[...]
    ks = ks_ref[...]
    fb = uc.shape[0]
    f = pl.program_id(0) * fb + lax.broadcasted_iota(jnp.int32, (fb, 1), 0)
    packed = f == 0
    wgt = jnp.where(packed, 1.0, 2.0) / (2 * cm_ref.shape[1])
    yc_ref[0] = (jnp.where(packed, uc * kc, uc * kc - us * ks) * wgt).astype(yc_ref.dtype)
    ys_ref[0] = (jnp.where(packed, us * ks, uc * ks + us * kc) * wgt).astype(ys_ref.dtype)


HY_DFT_ROWS = 512


def hy_fwd(u, cm, sm, kc, ks):
    B, L, C = u.shape
    fb = min(HY_DFT_ROWS, L)
    mat_spec = pl.BlockSpec((fb, L), lambda f, b: (f, 0))
    k_spec = pl.BlockSpec((fb, C), lambda f, b: (f, 0))
    y_spec = pl.BlockSpec((1, fb, C), lambda f, b: (b, f, 0))
    y_shape = jax.ShapeDtypeStruct((B, L, C), jnp.bfloat16)
    return pl.pallas_call(
        _hy_fwd_kernel,
        grid=(L // fb, B),
        in_specs=[mat_spec, mat_spec, pl.BlockSpec((1, L, C), lambda f, b: (b, 0, 0)), k_spec, k_spec],
        out_specs=[y_spec, y_spec],
        out_shape=[y_shape, y_shape],
        compiler_params=pltpu.CompilerParams(
            dimension_semantics=("parallel", "parallel"), vmem_limit_bytes=VMEM_LIMIT_BYTES),
        name="hy_fwd",
    )(cm, sm, u, kc, ks)


def _hy_inv_kernel(cm_ref, sm_ref, yc_ref, ys_ref, u_ref, gate_ref, skip_ref, o_ref):
    conv = (lax.dot_general(cm_ref[...], yc_ref[0], (_TN, ((), ())), preferred_element_type=jnp.float32)
            + lax.dot_general(sm_ref[...], ys_ref[0], (_TN, ((), ())), preferred_element_type=jnp.float32))
    o_ref[0] = (gate_ref[0] * (conv + u_ref[0] * skip_ref[...])).astype(o_ref.dtype)


HY_INV_COLS = 512


def hy_inv(yc, ys, cm, sm, u, u_col0, gate, gate_col0, skip, out_dtype):
    B, L, C = yc.shape
    tb = min(HY_DFT_ROWS, L)
    cols = HY_INV_COLS
    once = pl.Buffered(1)
    mat_spec = pl.BlockSpec((L, tb), lambda b, j, t: (0, t))
    y_spec = pl.BlockSpec((1, L, cols), lambda b, j, t: (b, 0, j), pipeline_mode=once)

    def win(col0):
        return pl.BlockSpec((1, tb, cols), lambda b, j, t: (b, t, col0 // cols + j))

    return pl.pallas_call(
        _hy_inv_kernel,
        grid=(B, C // cols, L // tb),
        in_specs=[mat_spec, mat_spec, y_spec, y_spec, win(u_col0), win(gate_col0),
                  pl.BlockSpec((1, cols), lambda b, j, t: (0, j))],
        out_specs=win(0),
        out_shape=jax.ShapeDtypeStruct((B, L, C), out_dtype),
        compiler_params=pltpu.CompilerParams(
            dimension_semantics=("parallel", "parallel", "arbitrary"), vmem_limit_bytes=VMEM_LIMIT_BYTES),
        name="hy_inv",
    )(cm, sm, yc, ys, u, gate, skip.reshape(1, C).astype(jnp.float32))


def hyena(proj, conv_w, conv_b, filt, skip, tables):
    B, L, _ = proj.shape
    cm, sm = tables
    hyu = hy_short(proj, conv_w, conv_b)
    kc, ks = hy_filter_spectrum(hyena_filters(L, *filt), cm, sm)
    v = hyu[..., :HY_W].astype(jnp.bfloat16)
    yc, ys = hy_fwd(v, cm, sm, kc[0], ks[0])
    z1 = hy_inv(yc, ys, cm, sm, hyu, 0, hyu, HY_W, skip[0], jnp.float32)
    yc, ys = hy_fwd(z1.astype(jnp.bfloat16), cm, sm, kc[1], ks[1])
    return hy_inv(yc, ys, cm, sm, z1, 0, hyu, 2 * HY_W, skip[1], jnp.bfloat16)


def mixer(proj, proj_c, p, log_f, log_b, s_f, s_b, tables, latent):
    hy = hyena(proj, p['hy_conv_w'], p['hy_conv_b'], p['hy_f'], p['hy_skip'], tables)
    ro, sf, sb = retention(proj, log_f, log_b, s_f, s_b, rope=latent)
    no = neighbourhood_attention(proj, proj_c, p['na_rpb'], p['q_norm'], p['k_norm'], local=latent)
    return mm(jnp.concatenate([hy, ro, no], axis=-1), p['w_out']), sf, sb


def _norm_mod_kernel(*refs, residual):
    if residual:
        x_ref, g_ref, shift_ref, scale_ref, res_ref, gate_ref, xo_ref, h_ref = refs
        x = x_ref[0] + gate_ref[0] * res_ref[0]
        xo_ref[0] = x
    else:
        x_ref, g_ref, shift_ref, scale_ref, h_ref = refs
        x = x_ref[0]
    y = x * lax.rsqrt(jnp.mean(x * x, axis=-1, keepdims=True) + EPS) * g_ref[...]
    h_ref[0] = (y * (1.0 + scale_ref[0]) + shift_ref[0]).astype(h_ref.dtype)


NORM_ROWS = 256


def norm_mod(x, g, shift, scale, res=None, gate=None):
    B, L, D = x.shape
    rb = min(NORM_ROWS, L)
    residual = res is not None

    def vec_spec(v):
        per_batch = v.shape[0] > 1
        return pl.BlockSpec((1, 1, D), lambda b, r: (b if per_batch else 0, 0, 0))

    row_spec = pl.BlockSpec((1, rb, D), lambda b, r: (b, r, 0))
    in_specs = [row_spec, pl.BlockSpec((1, D), lambda b, r: (0, 0)), vec_spec(shift), vec_spec(scale)]
    args = [x, g.reshape(1, D).astype(jnp.float32), shift, scale]
    h_shape = jax.ShapeDtypeStruct((B, L, D), jnp.bfloat16)
    if residual:
        in_specs += [row_spec, vec_spec(gate)]
        args += [res, gate]
        out_specs, out_shape = [row_spec, row_spec], [jax.ShapeDtypeStruct((B, L, D), jnp.float32), h_shape]
    else:
        out_specs, out_shape = row_spec, h_shape
    return pl.pallas_call(
        functools.partial(_norm_mod_kernel, residual=residual),
        grid=(B, L // rb),
        in_specs=in_specs,
        out_specs=out_specs,
        out_shape=out_shape,
        compiler_params=pltpu.CompilerParams(
            dimension_semantics=("parallel", "parallel"), vmem_limit_bytes=VMEM_LIMIT_BYTES),
        name="norm_mod",
    )(*args)


def layer(xl, xc, mod_l, mod_c, p, last, tables_l, tables_c):
    sa_l, ca_l, ga_l, sf_l, cf_l, gf_l = jnp.split(mod_l, N_MOD, axis=-1)
    mod_c_parts = jnp.split(mod_c, mod_c.shape[-1] // D_MODEL, axis=-1)
    sa_c, ca_c = mod_c_parts[0], mod_c_parts[1]
    log_f = -jnp.exp(p['ret_decay'][0].astype(jnp.float32))
    log_b = -jnp.exp(p['ret_decay'][1].astype(jnp.float32))
    w_in = p['w_in'].astype(jnp.bfloat16)
    hc = norm_mod(xc, p['norm_mix'], sa_c, ca_c)
    proj_c = mm(hc, w_in)
    zeros = jnp.zeros((xc.shape[0], RET_HEADS, HEAD_DIM, HEAD_DIM), jnp.float32)
    if last:
        _, s_f, s_b = retention(proj_c, log_f, log_b, zeros, zeros, rope=False)
    else:
        y_c, s_f, s_b = mixer(proj_c, proj_c, p, log_f, log_b, zeros, zeros, tables_c, False)
    hl = norm_mod(xl, p['norm_mix'], sa_l, ca_l)
    y_l, _, _ = mixer(mm(hl, w_in), proj_c, p, log_f, log_b, s_f, s_b, tables_l, True)
    xl, hf_l = norm_mod(xl, p['norm_ffn'], sf_l, cf_l, y_l, ga_l)
    xl = xl + gf_l * peer(hf_l, p['peer_wq'], p['peer_keys'], p['peer_u'], p['peer_v'])
    if last:
        return xl, None
    ga_c, sf_c, cf_c, gf_c = mod_c_parts[2], mod_c_parts[3], mod_c_parts[4], mod_c_parts[5]
    xc, hf_c = norm_mod(xc, p['norm_ffn'], sf_c, cf_c, y_c, ga_c)
    xc = xc + gf_c * peer(hf_c, p['peer_wq'], p['peer_keys'], p['peer_u'], p['peer_v'])
    return xl, xc


def kernel(x, c, ctx, c_ctx, w_mod, b_mod, norm_mix, w_in, hy_conv_w, hy_conv_b,
           hy_f_w1, hy_f_b1, hy_f_freq, hy_f_w2, hy_f_b2, hy_f_w3, hy_f_b3, hy_f_w4,
           hy_skip, ret_decay, na_rpb, q_norm, k_norm, w_out, norm_ffn,
           peer_wq, peer_keys, peer_u, peer_v):
    xl, xc = x, ctx
    sc = jax.nn.silu(c)
    scc = jax.nn.silu(c_ctx)
    tables_l = dft_tables(x.shape[1])
    tables_c = dft_tables(ctx.shape[1])
    for l in range(DEPTH):
        last = l == DEPTH - 1
        p = {
            'norm_mix': norm_mix[l], 'w_in': w_in[l],
            'hy_conv_w': hy_conv_w[l], 'hy_conv_b': hy_conv_b[l],
            'hy_f': (hy_f_w1[l], hy_f_b1[l], hy_f_freq[l], hy_f_w2[l], hy_f_b2[l],
                     hy_f_w3[l], hy_f_b3[l], hy_f_w4[l]),
            'hy_skip': hy_skip[l], 'ret_decay': ret_decay[l], 'na_rpb': na_rpb[l],
            'q_norm': q_norm[l], 'k_norm': k_norm[l], 'w_out': w_out[l],
            'norm_ffn': norm_ffn[l], 'peer_wq': peer_wq[l], 'peer_keys': peer_keys[l],
            'peer_u': peer_u[l].astype(jnp.bfloat16), 'peer_v': peer_v[l].astype(jnp.bfloat16),
        }
        mod_l = (sc @ w_mod[l] + b_mod[l])[:, None, :]
        n_ctx = 2 if last else N_MOD
        mod_c = (scc @ w_mod[l][:, :n_ctx * D_MODEL] + b_mod[l][:n_ctx * D_MODEL])[None, None, :]
        xl, xc = layer(xl, xc, mod_l, mod_c, p, last, tables_l, tables_c)
    return xl
```
